```python
import math
import jax, jax.numpy as jnp
from jax import lax
import numpy as np

D_MODEL = 4096
BATCH = 4
SEQ = 4096
DEPTH = 4

HEAD_DIM = 128
N_MIX_GROUPS = 4
D_MIX = D_MODEL
GROUP_WIDTH = D_MIX // N_MIX_GROUPS
GDN_HEADS = GROUP_WIDTH // HEAD_DIM
GDN_CONV = 4
GDN_CHUNK = 64
MLA_HEADS = GROUP_WIDTH // HEAD_DIM
MLA_Q_RANK = 896
MLA_KV_RANK = 512
MLA_NOPE_DIM = 128
MLA_ROPE_DIM = 64
MLA_V_DIM = 128
ROPE_THETA = 10000.0
SGU_CHUNK = 128
SGU_GROUPS = 8
SGU_GROUP_DIM = GROUP_WIDTH // SGU_GROUPS
SB_HEADS = GROUP_WIDTH // HEAD_DIM
Q_BLOCK = 128
D_FF = 11008
N_EXPERTS = 8
TOP_K = 2
D_EXPERT = 1792
N_DENSE = (DEPTH + 1) // 2
N_MOE = DEPTH // 2
DEEPNORM_ALPHA = (2 * DEPTH) ** 0.25
DEEPNORM_BETA = (8 * DEPTH) ** -0.25
IN_SIZES = (3 * GROUP_WIDTH, GROUP_WIDTH, GDN_HEADS, GDN_HEADS, MLA_Q_RANK, MLA_KV_RANK, MLA_ROPE_DIM, 2 * GROUP_WIDTH, 3 * GROUP_WIDTH)
D_IN = sum(IN_SIZES)

kernel_name = 'hybrid_parallel_mixer_deepnorm_block'


def _layer_norm(x, w, b, eps=1e-5):
    xf = x.astype(jnp.float32)
    mu = jnp.mean(xf, axis=-1, keepdims=True)
    var = jnp.mean(jnp.square(xf - mu), axis=-1, keepdims=True)
    return ((xf - mu) * lax.rsqrt(var + eps) * w.astype(jnp.float32) + b.astype(jnp.float32)).astype(x.dtype)


def _rms_norm(x, w, eps=1e-6):
    xf = x.astype(jnp.float32)
    y = xf * lax.rsqrt(jnp.mean(xf * xf, axis=-1, keepdims=True) + eps)
    return (y * w.astype(jnp.float32)).astype(x.dtype)


def _l2_normalize(x, eps=1e-6):
    xf = x.astype(jnp.float32)
    return xf * lax.rsqrt(jnp.sum(xf * xf, axis=-1, keepdims=True) + eps)


def _apply_rope(x, cos, sin):
    half = x.shape[-1] // 2
    xf = x.astype(jnp.float32)
    x1, x2 = xf[..., :half], xf[..., half:]
    return jnp.concatenate([x1 * cos - x2 * sin, x2 * cos + x1 * sin], axis=-1).astype(x.dtype)


def _causal_depthwise_conv(x, w):
    k, c = w.shape
    return lax.conv_general_dilated(x, w[:, None, :], window_strides=(1,), padding=[(k - 1, 0)],
                                    dimension_numbers=('NWC', 'WIO', 'NWC'), feature_group_count=c)


def _gated_delta_rule(q, k, v, g, beta):
    b, h, s, dk = q.shape
    dv = v.shape[-1]
    c = GDN_CHUNK
    n = s // c
    q = q * dk ** -0.5
    q, k, v = (t.reshape(b, h, n, c, t.shape[-1]) for t in (q, k, v))
    g = jnp.cumsum(g.reshape(b, h, n, c), axis=-1)
    beta = beta.reshape(b, h, n, c)
    incl = jnp.tril(jnp.ones((c, c), dtype=bool))
    strict = jnp.tril(jnp.ones((c, c), dtype=bool), -1)
    decay = jnp.exp(jnp.where(incl, g[..., :, None] - g[..., None, :], -jnp.inf))
    k_beta = k * beta[..., None]
    a_mat = jnp.where(strict, jnp.einsum('bhnid,bhnjd->bhnij', k_beta, k) * decay, 0.0)
    rhs = jnp.concatenate([k_beta * jnp.exp(g)[..., None], v * beta[..., None]], axis=-1)
    sol = lax.linalg.triangular_solve(a_mat + jnp.eye(c, dtype=a_mat.dtype), rhs,
                                      left_side=True, lower=True, unit_diagonal=True)
    w_chunk, u_chunk = sol[..., :dk], sol[..., dk:]
    qk = jnp.einsum('bhnid,bhnjd->bhnij', q, k) * decay
    q_dec = q * jnp.exp(g)[..., None]
    g_last = g[..., -1]
    k_dec = k * jnp.exp(g_last[..., None] - g)[..., None]

    def chunk_step(state, xs):
        w_c, u_c, qk_c, q_c, k_c, gl = xs
        v_new = u_c - jnp.einsum('bhcd,bhde->bhce', w_c, state)
        out = jnp.einsum('bhcd,bhde->bhce', q_c, state) + jnp.einsum('bhij,bhje->bhie', qk_c, v_new)
        state = state * jnp.exp(gl)[..., None, None] + jnp.einsum('bhcd,bhce->bhde', k_c, v_new)
        return state, out

    xs = tuple(jnp.moveaxis(t, 2, 0) for t in (w_chunk, u_chunk, qk, q_dec, k_dec, g_last))
    state0 = jnp.zeros((b, h, dk, dv), jnp.float32)
    _, out = lax.scan(chunk_step, state0, xs)
    return jnp.moveaxis(out, 0, 2).reshape(b, h, s, dv)


def _mla_attention(q_nope, q_rope, k_nope, k_rope, v):
    s = q_nope.shape[2]
    scale = (MLA_NOPE_DIM + MLA_ROPE_DIM) ** -0.5
    outs = []
    for blk in range(s // Q_BLOCK):
        q0, q1 = blk * Q_BLOCK, (blk + 1) * Q_BLOCK
        sc = (jnp.einsum('bhqd,bhkd->bhqk', q_nope[:, :, q0:q1], k_nope[:, :, :q1])
              + jnp.einsum('bhqd,bkd->bhqk', q_rope[:, :, q0:q1], k_rope[:, :q1])).astype(jnp.float32) * scale
        mask = (q0 + jnp.arange(Q_BLOCK))[:, None] >= jnp.arange(q1)[None, :]
        p = jax.nn.softmax(jnp.where(mask, sc, -jnp.inf), axis=-1)
        outs.append(jnp.einsum('bhqk,bhkd->bhqd', p.astype(v.dtype), v[:, :, :q1]))
    return jnp.concatenate(outs, axis=2)


def _stick_breaking_attention(q, k, v):
    s = q.shape[2]
    scale = q.shape[-1] ** -0.5
    outs = []
    for blk in range(s // Q_BLOCK):
        q0, q1 = blk * Q_BLOCK, (blk + 1) * Q_BLOCK
        z = jnp.einsum('bhqd,bhkd->bhqk', q[:, :, q0:q1], k[:, :, :q1]).astype(jnp.float32) * scale
        strict = jnp.arange(q1)[None, :] < (q0 + jnp.arange(Q_BLOCK))[:, None]
        log_beta = jax.nn.log_sigmoid(z)
        log_rest = jnp.where(strict, jax.nn.log_sigmoid(-z), 0.0)
        tail = lax.cumsum(log_rest, axis=3, reverse=True) - log_rest
        a = jnp.where(strict, jnp.exp(log_beta + tail), 0.0)
        outs.append(jnp.einsum('bhqk,bhkd->bhqd', a.astype(v.dtype), v[:, :, :q1]))
    return jnp.concatenate(outs, axis=2)


def _token_mixer(x, w_in, conv_w, a_log, dt_bias, gdn_norm_w, q_norm_w, w_uq, kv_norm_w, w_ukv,
                 sgu_norm_w, sgu_norm_b, w_s, b_s, w_out, cos, sin):
    b, s, _ = x.shape
    proj = jnp.einsum('bsd,dc->bsc', x, w_in)
    split_points = np.cumsum(IN_SIZES)[:-1].tolist()
    gdn_qkv, gdn_z, gdn_a, gdn_b, mla_cq, mla_ckv, mla_kr, sgu_uv, sb_qkv = jnp.split(proj, split_points, axis=-1)
    to_heads = lambda t, nh: t.reshape(b, s, nh, -1).transpose(0, 2, 1, 3)
    from_heads = lambda t: t.transpose(0, 2, 1, 3).reshape(b, s, -1).astype(x.dtype)

    qkv = jax.nn.silu(_causal_depthwise_conv(gdn_qkv, conv_w))
    gq, gk, gv = jnp.split(qkv, 3, axis=-1)
    gq = _l2_normalize(to_heads(gq, GDN_HEADS))
    gk = _l2_normalize(to_heads(gk, GDN_HEADS))
    gv = to_heads(gv, GDN_HEADS).astype(jnp.float32)
    beta = jax.nn.sigmoid(gdn_b.astype(jnp.float32)).transpose(0, 2, 1)
    g = (-jnp.exp(a_log.astype(jnp.float32))
         * jax.nn.softplus(gdn_a.astype(jnp.float32) + dt_bias.astype(jnp.float32))).transpose(0, 2, 1)
    o_a = _gated_delta_rule(gq, gk, gv, g, beta)
    o_a = _rms_norm(o_a, gdn_norm_w) * jax.nn.silu(to_heads(gdn_z, GDN_HEADS).astype(jnp.float32))
    o_a = from_heads(o_a)

    cq = _rms_norm(mla_cq, q_norm_w)
    mq = to_heads(jnp.einsum('bsr,rc->bsc', cq, w_uq), MLA_HEADS)
    q_nope, q_rope = mq[..., :MLA_NOPE_DIM], _apply_rope(mq[..., MLA_NOPE_DIM:], cos, sin)
    ckv = _rms_norm(mla_ckv, kv_norm_w)
    mkv = to_heads(jnp.einsum('bsr,rc->bsc', ckv, w_ukv), MLA_HEADS)
    k_nope, mv = mkv[..., :MLA_NOPE_DIM], mkv[..., MLA_NOPE_DIM:]
    k_rope = _apply_rope(mla_kr, cos, sin)
    o_b = from_heads(_mla_attention(q_nope, q_rope, k_nope, k_rope, mv))

    u, sv = jnp.split(jax.nn.gelu(sgu_uv, approximate=False), 2, axis=-1)
    sv = _layer_norm(sv, sgu_norm_w, sgu_norm_b)
    sv = sv.reshape(b, s // SGU_CHUNK, SGU_CHUNK, SGU_GROUPS, SGU_GROUP_DIM)
    causal = jnp.tril(jnp.ones((SGU_CHUNK, SGU_CHUNK), dtype=bool))
    w_mix = jnp.where(causal, w_s, 0.0)
    mixed = jnp.einsum('gts,bnsgc->bntgc', w_mix, sv) + b_s.T[:, :, None]
    o_c = (u * mixed.reshape(b, s, GROUP_WIDTH)).astype(x.dtype)

    sq, sk, svv = (to_heads(t, SB_HEADS) for t in jnp.split(sb_qkv, 3, axis=-1))
    o_d = from_heads(_stick_breaking_attention(sq, sk, svv))

    heads = jnp.concatenate([o_a, o_b, o_c, o_d], axis=-1)
    return jnp.einsum('bsc,cd->bsd', heads, w_out)


def _swiglu(h, w_gate, w_up, w_down):
    return (jax.nn.silu(h @ w_gate) * (h @ w_up)) @ w_down


def _moe_swiglu(h, router_w, router_b, w_gate, w_up, w_down):
    b, s, d = h.shape
    t = h.reshape(b * s, d)
    logits = (t @ router_w).astype(jnp.float32) + router_b.astype(jnp.float32)
    top_logits, top_idx = lax.top_k(logits, TOP_K)
    top_w = jax.nn.softmax(top_logits, axis=-1)
    gates = jnp.einsum('tke,tk->te', jax.nn.one_hot(top_idx, N_EXPERTS, dtype=jnp.float32), top_w)
    out = jnp.zeros_like(t)
    for e in range(N_EXPERTS):
        out = out + gates[:, e:e + 1].astype(t.dtype) * _swiglu(t, w_gate[e], w_up[e], w_down[e])
    return out.reshape(b, s, d)


def setup_inputs(seed: int = 0) -> dict:
    key = jax.random.key(seed)
    ks = jax.random.split(key, 32)
    L = DEPTH
    f32 = jnp.float32
    nrm = lambda k, shape, scale: jax.random.normal(k, shape, f32) * scale
    x = nrm(ks[0], (BATCH, SEQ, D_MODEL), 1.0)
    w_in = nrm(ks[1], (L, D_MODEL, D_IN), D_MODEL ** -0.5)
    gdn_conv_w = nrm(ks[2], (L, GDN_CONV, 3 * GROUP_WIDTH), GDN_CONV ** -0.5)
    gdn_a_log = jnp.log(jax.random.uniform(ks[3], (L, GDN_HEADS), f32, 1.0, 16.0))
    dt = jnp.exp(jax.random.uniform(ks[4], (L, GDN_HEADS), f32, math.log(1e-3), math.log(1e-1)))
    gdn_dt_bias = dt + jnp.log(-jnp.expm1(-dt))
    gdn_norm_w = 1.0 + nrm(ks[5], (L, HEAD_DIM), 0.1)
    mla_q_norm_w = 1.0 + nrm(ks[6], (L, MLA_Q_RANK), 0.1)
    mla_w_uq = nrm(ks[7], (L, MLA_Q_RANK, MLA_HEADS * (MLA_NOPE_DIM + MLA_ROPE_DIM)), MLA_Q_RANK ** -0.5)
    mla_kv_norm_w = 1.0 + nrm(ks[8], (L, MLA_KV_RANK), 0.1)
    mla_w_ukv = nrm(ks[9], (L, MLA_KV_RANK, MLA_HEADS * (MLA_NOPE_DIM + MLA_V_DIM)), MLA_KV_RANK ** -0.5)
    sgu_norm_w = 1.0 + nrm(ks[10], (L, GROUP_WIDTH), 0.1)
    sgu_norm_b = nrm(ks[11], (L, GROUP_WIDTH), 0.01)
    sgu_w_s = nrm(ks[12], (L, SGU_GROUPS, SGU_CHUNK, SGU_CHUNK), SGU_CHUNK ** -0.5)
    sgu_b_s = 1.0 + nrm(ks[13], (L, SGU_GROUPS, SGU_CHUNK), 0.1)
    w_out = nrm(ks[14], (L, D_MIX, D_MODEL), D_MIX ** -0.5 * DEEPNORM_BETA)
    ln_mix_w = 1.0 + nrm(ks[15], (L, D_MODEL), 0.1)
    ln_mix_b = nrm(ks[16], (L, D_MODEL), 0.01)
    ffn_w_gate = nrm(ks[17], (N_DENSE, D_MODEL, D_FF), D_MODEL ** -0.5)
    ffn_w_up = nrm(ks[18], (N_DENSE, D_MODEL, D_FF), D_MODEL ** -0.5)
    ffn_w_down = nrm(ks[19], (N_DENSE, D_FF, D_MODEL), D_FF ** -0.5 * DEEPNORM_BETA)
    moe_router_w = nrm(ks[20], (N_MOE, D_MODEL, N_EXPERTS), D_MODEL ** -0.5)
    moe_router_b = nrm(ks[21], (N_MOE, N_EXPERTS), 0.01)
    moe_w_gate = nrm(ks[22], (N_MOE, N_EXPERTS, D_MODEL, D_EXPERT), D_MODEL ** -0.5)
    moe_w_up = nrm(ks[23], (N_MOE, N_EXPERTS, D_MODEL, D_EXPERT), D_MODEL ** -0.5)
    moe_w_down = nrm(ks[24], (N_MOE, N_EXPERTS, D_EXPERT, D_MODEL), D_EXPERT ** -0.5 * DEEPNORM_BETA)
    ln_ffn_w = 1.0 + nrm(ks[25], (L, D_MODEL), 0.1)
    ln_ffn_b = nrm(ks[26], (L, D_MODEL), 0.01)
    return {'x': x, 'w_in': w_in, 'gdn_conv_w': gdn_conv_w, 'gdn_a_log': gdn_a_log, 'gdn_dt_bias': gdn_dt_bias,
            'gdn_norm_w': gdn_norm_w, 'mla_q_norm_w': mla_q_norm_w, 'mla_w_uq': mla_w_uq,
            'mla_kv_norm_w': mla_kv_norm_w, 'mla_w_ukv': mla_w_ukv, 'sgu_norm_w': sgu_norm_w,
            'sgu_norm_b': sgu_norm_b, 'sgu_w_s': sgu_w_s, 'sgu_b_s': sgu_b_s, 'w_out': w_out,
            'ln_mix_w': ln_mix_w, 'ln_mix_b': ln_mix_b, 'ffn_w_gate': ffn_w_gate, 'ffn_w_up': ffn_w_up,
            'ffn_w_down': ffn_w_down, 'moe_router_w': moe_router_w, 'moe_router_b': moe_router_b,
            'moe_w_gate': moe_w_gate, 'moe_w_up': moe_w_up, 'moe_w_down': moe_w_down,
            'ln_ffn_w': ln_ffn_w, 'ln_ffn_b': ln_ffn_b}


def reference(x, w_in, gdn_conv_w, gdn_a_log, gdn_dt_bias, gdn_norm_w, mla_q_norm_w, mla_w_uq,
              mla_kv_norm_w, mla_w_ukv, sgu_norm_w, sgu_norm_b, sgu_w_s, sgu_b_s, w_out,
              ln_mix_w, ln_mix_b, ffn_w_gate, ffn_w_up, ffn_w_down, moe_router_w, moe_router_b,
              moe_w_gate, moe_w_up, moe_w_down, ln_ffn_w, ln_ffn_b):
    pos = jnp.arange(x.shape[1], dtype=jnp.float32)
    half = MLA_ROPE_DIM // 2
    inv_freq = ROPE_THETA ** (-jnp.arange(half, dtype=jnp.float32) / half)
    ang = pos[:, None] * inv_freq[None, :]
    cos, sin = jnp.cos(ang), jnp.sin(ang)
    for layer in range(DEPTH):
        mix = _token_mixer(x, w_in[layer], gdn_conv_w[layer], gdn_a_log[layer], gdn_dt_bias[layer],
                           gdn_norm_w[layer], mla_q_norm_w[layer], mla_w_uq[layer], mla_kv_norm_w[layer],
                           mla_w_ukv[layer], sgu_norm_w[layer], sgu_norm_b[layer], sgu_w_s[layer],
                           sgu_b_s[layer], w_out[layer], cos, sin)
        x = _layer_norm(DEEPNORM_ALPHA * x + mix, ln_mix_w[layer], ln_mix_b[layer])
        i = layer // 2
        if layer % 2 == 0:
            ffn = _swiglu(x, ffn_w_gate[i], ffn_w_up[i], ffn_w_down[i])
        else:
            ffn = _moe_swiglu(x, moe_router_w[i], moe_router_b[i], moe_w_gate[i], moe_w_up[i], moe_w_down[i])
        x = _layer_norm(DEEPNORM_ALPHA * x + ffn, ln_ffn_w[layer], ln_ffn_b[layer])
    return x
```

```python
import functools
import math

import jax
import jax.numpy as jnp
import numpy as np
from jax import lax
from jax.experimental import pallas as pl
from jax.experimental.pallas import tpu as pltpu

F32 = jnp.float32
BF16 = jnp.bfloat16
HIGHEST = lax.Precision.HIGHEST

D_MODEL = 4096
DEPTH = 4
HEAD_DIM = 128
GROUP_WIDTH = 1024
N_HEADS = 8
GDN_CONV = 4
GDN_CHUNK = 64
MLA_Q_RANK = 896
MLA_KV_RANK = 512
MLA_NOPE_DIM = 128
MLA_ROPE_DIM = 64
ROPE_THETA = 10000.0
SGU_CHUNK = 128
SGU_GROUPS = 8
D_FF = 11008
N_EXPERTS = 8
D_EXPERT = 1792
DEEPNORM_ALPHA = (2 * DEPTH) ** 0.25

LANES = 128
SUBLANES = 8
VMEM_LIMIT_BYTES = 56 * 1024 * 1024

OFF_GDN_QKV = 0
OFF_GDN_Z = 3072
OFF_SGU = 4096
OFF_SB = 6144
OFF_MLA = 9216
OFF_MISC = OFF_MLA + MLA_Q_RANK
D_IN_PAD = 10752
MISC_A = 64
MISC_B = 72
D_FF_PAD = 11264


def _cparams(*sem):
    return pltpu.CompilerParams(dimension_semantics=sem, vmem_limit_bytes=VMEM_LIMIT_BYTES)


def _dot(a, b):
    return jnp.dot(a, b, preferred_element_type=F32)


def _dot_nt(a, b, precision=None):
    return lax.dot_general(a, b, (((1,), (1,)), ((), ())), preferred_element_type=F32, precision=precision)


def _dot_tn(a, b):
    return lax.dot_general(a, b, (((0,), (0,)), ((), ())), preferred_element_type=F32)


def _silu(x):
    return x * jax.nn.sigmoid(x)


def _mm_kernel(*refs, n_a, nk, alpha, has_res):
    a_refs = refs[:n_a]
    w_refs = refs[n_a:2 * n_a]
    pos = 2 * n_a
    res_ref = refs[pos] if has_res else None
    pos += int(has_res)
    o_ref = refs[pos]
    acc_ref = refs[pos + 1] if nk > 1 else None

    def partial_product():
        acc = _dot(a_refs[0][...], w_refs[0][...])
        for a_ref, w_ref in zip(a_refs[1:], w_refs[1:]):
            acc = acc + _dot(a_ref[...], w_ref[...])
        return acc

    def epilogue(acc):
        if has_res:
            acc = acc + alpha * res_ref[...]
        o_ref[...] = acc.astype(o_ref.dtype)

    if nk == 1:
        epilogue(partial_product())
    else:
        k = pl.program_id(2)

        @pl.when(k == 0)
        def _():
            acc_ref[...] = jnp.zeros_like(acc_ref)

        acc_ref[...] += partial_product()

        @pl.when(k == nk - 1)
        def _():
            epilogue(acc_ref[...])


def _matmul(a_list, w, *, tm, tn, tk, out_dtype, res=None, alpha=None, name="matmul"):
    n_a = len(a_list)
    m, k_seg = a_list[0].shape
    n = w.shape[-1]
    tm, tn, tk = min(tm, m), min(tn, n), min(tk, k_seg)
    nk = k_seg // tk
    assert m % tm == 0 and n % tn == 0 and k_seg % tk == 0
    assert n_a == 1 or nk == 1
    in_specs = [pl.BlockSpec((tm, tk), lambda i, j, k: (i, k)) for _ in a_list]
    in_specs += [pl.BlockSpec((tk, tn), functools.partial(lambda i, j, k, s: (s * nk + k, j), s=s))
                 for s in range(n_a)]
    operands = list(a_list) + [w] * n_a
    if res is not None:
        in_specs.append(pl.BlockSpec((tm, tn), lambda i, j, k: (i, j)))
        operands.append(res)
    return pl.pallas_call(
        functools.partial(_mm_kernel, n_a=n_a, nk=nk, alpha=alpha, has_res=res is not None),
        grid=(m // tm, n // tn, nk),
        in_specs=in_specs,
        out_specs=pl.BlockSpec((tm, tn), lambda i, j, k: (i, j)),
        out_shape=jax.ShapeDtypeStruct((m, n), out_dtype),
        scratch_shapes=[pltpu.VMEM((tm, tn), F32)] if nk > 1 else [],
        compiler_params=_cparams("parallel", "parallel", "arbitrary"),
        name=name,
    )(*operands)


def _gate_up_kernel(*refs, tiles_per_expert, has_gates):
    if has_gates:
        x_ref, wg_ref, wu_ref, gates_ref, o_ref = refs
    else:
        x_ref, wg_ref, wu_ref, o_ref = refs
    x = x_ref[...]
    h = _silu(_dot(x, wg_ref[...])) * _dot(x, wu_ref[...])
    if has_gates:
        e = pl.program_id(1) // tiles_per_expert
        gates = gates_ref[...]
        lane = lax.broadcasted_iota(jnp.int32, gates.shape, 1)
        h = h * jnp.sum(jnp.where(lane == e, gates, 0.0), axis=1, keepdims=True)
    o_ref[...] = h.astype(o_ref.dtype)


def _gate_up(x, wg, wu, *, tm, tn, gates=None, name="gate_up"):
    m, k = x.shape
    n_exp, _, n_e = wg.shape
    tm, tn = min(tm, m), min(tn, n_e)
    assert m % tm == 0 and n_e % tn == 0
    tpe = n_e // tn
    w_spec = pl.BlockSpec((None, k, tn), lambda i, j: (j // tpe, 0, j % tpe))
    in_specs = [pl.BlockSpec((tm, k), lambda i, j: (i, 0)), w_spec, w_spec]
    operands = [x, wg, wu]
    if gates is not None:
        in_specs.append(pl.BlockSpec((tm, LANES), lambda i, j: (i, 0)))
        operands.append(gates)
    return pl.pallas_call(
        functools.partial(_gate_up_kernel, tiles_per_expert=tpe, has_gates=gates is not None),
        grid=(m // tm, n_exp * tpe),
        in_specs=in_specs,
        out_specs=pl.BlockSpec((tm, tn), lambda i, j: (i, j)),
        out_shape=jax.ShapeDtypeStruct((m, n_exp * n_e), BF16),
        compiler_params=_cparams("parallel", "arbitrary"),
        name=name,
    )(*operands)


def _ln_kernel(x_ref, w_ref, b_ref, o_ref, ob_ref):
    x = x_ref[...]
    mu = jnp.mean(x, axis=-1, keepdims=True)
    xc = x - mu
    var = jnp.mean(xc * xc, axis=-1, keepdims=True)
    y = xc * lax.rsqrt(var + 1e-5) * w_ref[...] + b_ref[...]
    o_ref[...] = y
    ob_ref[...] = y.astype(BF16)


def _layer_norm(x, w, b, *, tm=256, name="layer_norm"):
    m, d = x.shape
    tm = min(tm, m)
    row = pl.BlockSpec((tm, d), lambda i: (i, 0))
    vec = pl.BlockSpec((1, d), lambda i: (0, 0))
    return pl.pallas_call(
        _ln_kernel,
        grid=(m // tm,),
        in_specs=[row, vec, vec],
        out_specs=[row, row],
        out_shape=[jax.ShapeDtypeStruct((m, d), F32), jax.ShapeDtypeStruct((m, d), BF16)],
        compiler_params=_cparams("parallel"),
        name=name,
    )(x, w.reshape(1, d), b.reshape(1, d))


def _router_kernel(x_ref, w_ref, b_ref, o_ref):
    logits = jnp.dot(x_ref[...], w_ref[...], preferred_element_type=F32, precision=HIGHEST) + b_ref[...]
    lane = lax.broadcasted_iota(jnp.int32, logits.shape, 1)
    neg = jnp.float32(-jnp.inf)
    logits = jnp.where(lane < N_EXPERTS, logits, neg)
    m1 = jnp.max(logits, axis=1, keepdims=True)
    i1 = jnp.min(jnp.where(logits == m1, lane, LANES), axis=1, keepdims=True)
    rest = jnp.where(lane == i1, neg, logits)
    m2 = jnp.max(rest, axis=1, keepdims=True)
    i2 = jnp.min(jnp.where(rest == m2, lane, LANES), axis=1, keepdims=True)
    e2 = jnp.exp(m2 - m1)
    denom = 1.0 + e2
    o_ref[...] = jnp.where(lane == i1, 1.0 / denom, 0.0) + jnp.where(lane == i2, e2 / denom, 0.0)


def _router(x, w, b, *, tm=512, name="router"):
    m, d = x.shape
    tm = min(tm, m)
    w_pad = jnp.pad(w, ((0, 0), (0, LANES - N_EXPERTS)))
    b_pad = jnp.pad(b, (0, LANES - N_EXPERTS)).reshape(1, LANES)
    return pl.pallas_call(
        _router_kernel,
        grid=(m // tm,),
        in_specs=[pl.BlockSpec((tm, d), lambda i: (i, 0)),
                  pl.BlockSpec((d, LANES), lambda i: (0, 0)),
                  pl.BlockSpec((1, LANES), lambda i: (0, 0))],
        out_specs=pl.BlockSpec((tm, LANES), lambda i: (i, 0)),
        out_shape=jax.ShapeDtypeStruct((m, LANES), F32),
        compiler_params=_cparams("parallel"),
        name=name,
    )(x, w_pad, b_pad)


_CONV_HALO = SUBLANES


def _gdn_prep_kernel(x_ref, w_ref, o_ref, buf_ref, *, rows):
    @pl.when(pl.program_id(1) == 0)
    def _():
        buf_ref[0:_CONV_HALO, :] = jnp.zeros((_CONV_HALO, buf_ref.shape[1]), F32)

    buf_ref[_CONV_HALO:_CONV_HALO + rows, :] = x_ref[...].astype(F32)
    n_strips = x_ref.shape[1] // HEAD_DIM
    for s in range(n_strips):
        cols = slice(s * HEAD_DIM, (s + 1) * HEAD_DIM)
        acc = w_ref[GDN_CONV - 1:GDN_CONV, cols] * buf_ref[_CONV_HALO:_CONV_HALO + rows, cols]
        for i in range(GDN_CONV - 1):
            start = _CONV_HALO - (GDN_CONV - 1) + i
            acc = acc + w_ref[i:i + 1, cols] * buf_ref[start:start + rows, cols]
        y = _silu(acc)
        if s < 2 * N_HEADS:
            y = y * lax.rsqrt(jnp.sum(y * y, axis=-1, keepdims=True) + 1e-6)
            if s < N_HEADS:
                y = y * HEAD_DIM ** -0.5
        o_ref[:, cols] = y
    buf_ref[0:_CONV_HALO, :] = buf_ref[rows:rows + _CONV_HALO, :]


def _gdn_prep(proj, conv_w, *, batch, seq, rows=256):
    rows = min(rows, seq)
    width = 3 * GROUP_WIDTH
    nblk = seq // rows
    return pl.pallas_call(
        functools.partial(_gdn_prep_kernel, rows=rows),
        grid=(batch, nblk),
        in_specs=[pl.BlockSpec((rows, width), lambda b, s: (b * nblk + s, OFF_GDN_QKV // width)),
                  pl.BlockSpec((GDN_CONV, width), lambda b, s: (0, 0))],
        out_specs=pl.BlockSpec((rows, width), lambda b, s: (b * nblk + s, 0)),
        out_shape=jax.ShapeDtypeStruct((batch * seq, width), F32),
        scratch_shapes=[pltpu.VMEM((rows + _CONV_HALO, width), F32)],
        compiler_params=_cparams("parallel", "arbitrary"),
        name="gdn_prep",
    )(proj, conv_w)


def _gdn_kernel(q_ref, k_ref, v_ref, z_ref, misc_ref, alog_ref, dtb_ref, nw_ref, o_ref, s_ref, *, n_chunks):
    c_len = GDN_CHUNK
    h = pl.program_id(1)

    @pl.when(pl.program_id(2) == 0)
    def _():
        s_ref[...] = jnp.zeros_like(s_ref)

    r_io = lax.broadcasted_iota(jnp.int32, (c_len, c_len), 0)
    c_io = lax.broadcasted_iota(jnp.int32, (c_len, c_len), 1)
    incl = r_io >= c_io
    strict = r_io > c_io
    l_incl = incl.astype(F32)
    eye = (r_io == c_io).astype(F32)
    sel = lax.broadcasted_iota(jnp.int32, (LANES, LANES), 0)
    pick_a = (sel == MISC_A + h).astype(BF16)
    pick_b = (sel == MISC_B + h).astype(BF16)
    lane0 = (lax.broadcasted_iota(jnp.int32, (c_len, LANES), 1) == 0).astype(F32)
    neg_a = -jnp.exp(alog_ref[...])
    dt_bias = dtb_ref[...]

    def hdot(a, b):
        return jnp.dot(a, b, preferred_element_type=F32, precision=HIGHEST)

    for c in range(n_chunks):
        rows = pl.ds(c * c_len, c_len)
        misc = misc_ref[rows, :]
        xg = _dot(misc, pick_a) + dt_bias
        g_b = neg_a * (jnp.maximum(xg, 0.0) + jnp.log1p(jnp.exp(-jnp.abs(xg))))
        beta_b = jax.nn.sigmoid(_dot(misc, pick_b))
        gc_b = hdot(l_incl, g_b)
        gc_row = _dot_nt(lane0, gc_b, precision=HIGHEST)
        decay = jnp.where(incl, jnp.exp(gc_b[:, :c_len] - gc_row), 0.0)

        q = q_ref[rows, :]
        k = k_ref[rows, :]
        v = v_ref[rows, :]
        k_bf = k.astype(BF16)
        kb = k * beta_b
        a_mat = jnp.where(strict, _dot_nt(kb.astype(BF16), k_bf) * decay, 0.0)
        t_inv = eye - a_mat
        a_pow = hdot(a_mat, a_mat)
        n_doublings = int(math.log2(c_len)) - 1
        for it in range(n_doublings):
            t_inv = t_inv + hdot(t_inv, a_pow)
            if it < n_doublings - 1:
                a_pow = hdot(a_pow, a_pow)
        eg = jnp.exp(gc_b)
        sol = hdot(t_inv, jnp.concatenate([kb * eg, v * beta_b], axis=1))
        w_c, u_c = sol[:, :HEAD_DIM], sol[:, HEAD_DIM:]
        qk = _dot_nt(q.astype(BF16), k_bf) * decay
        g_last = gc_b[c_len - 1:c_len, :]
        k_dec = k * jnp.exp(g_last - gc_b)

        state = s_ref[...]
        state_bf = state.astype(BF16)
        v_new = u_c - _dot(w_c.astype(BF16), state_bf)
        v_new_bf = v_new.astype(BF16)
        out = _dot((q * eg).astype(BF16), state_bf) + _dot(qk.astype(BF16), v_new_bf)
        s_ref[...] = state * jnp.exp(g_last) + _dot_tn(k_dec.astype(BF16), v_new_bf)

        y = out * lax.rsqrt(jnp.mean(out * out, axis=-1, keepdims=True) + 1e-6) * nw_ref[...]
        o_ref[rows, :] = (y * _silu(z_ref[rows, :].astype(F32))).astype(BF16)


def _gdn(qkv, proj, a_log, dt_bias, norm_w, *, batch, seq, rows=256):
    rows = min(rows, seq)
    nblk = seq // rows
    hb = GROUP_WIDTH // HEAD_DIM

    def head_spec(col0):
        return pl.BlockSpec((rows, HEAD_DIM), lambda b, h, s: (b * nblk + s, col0 + h))

    per_head = pl.BlockSpec((None, 1, LANES), lambda b, h, s: (h, 0, 0))
    bcast = lambda t: jnp.broadcast_to(t.astype(F32)[:, None, None], (N_HEADS, 1, LANES))
    return pl.pallas_call(
        functools.partial(_gdn_kernel, n_chunks=rows // GDN_CHUNK),
        grid=(batch, N_HEADS, nblk),
        in_specs=[head_spec(0), head_spec(hb), head_spec(2 * hb),
                  head_spec(OFF_GDN_Z // HEAD_DIM),
                  pl.BlockSpec((rows, LANES), lambda b, h, s: (b * nblk + s, OFF_MISC // LANES)),
                  per_head, per_head,
                  pl.BlockSpec((1, HEAD_DIM), lambda b, h, s: (0, 0))],
        out_specs=head_spec(0),
        out_shape=jax.ShapeDtypeStruct((batch * seq, GROUP_WIDTH), BF16),
        scratch_shapes=[pltpu.VMEM((HEAD_DIM, HEAD_DIM), F32)],
        compiler_params=_cparams("parallel", "parallel", "arbitrary"),
        name="gdn",
    )(qkv, qkv, qkv, proj, proj, bcast(a_log), bcast(dt_bias), norm_w.reshape(1, HEAD_DIM))


_MLA_QK_PAD = 2 * HEAD_DIM
_MLA_IN = MLA_Q_RANK + LANES + MLA_KV_RANK


def _mla_prep_kernel(x_ref, qnw_ref, kvnw_ref, wuq_ref, wukv_ref, cos_ref, sin_ref, q_ref, k_ref, v_ref):
    x = x_ref[...]
    cq = x[:, :MLA_Q_RANK].astype(F32)
    misc = x[:, MLA_Q_RANK:MLA_Q_RANK + LANES].astype(F32)
    ckv = x[:, MLA_Q_RANK + LANES:].astype(F32)

    def rms(t, w):
        return t * lax.rsqrt(jnp.mean(t * t, axis=-1, keepdims=True) + 1e-6) * w

    mq = _dot(rms(cq, qnw_ref[...]).astype(BF16), wuq_ref[...])
    mkv = _dot(rms(ckv, kvnw_ref[...]).astype(BF16), wukv_ref[...])
    cos = cos_ref[...]
    sin = sin_ref[...]
    lane = lax.broadcasted_iota(jnp.int32, cos.shape, 1)
    first_half = (lane % MLA_ROPE_DIM) < MLA_ROPE_DIM // 2
    low = lane < MLA_ROPE_DIM

    def rope(t):
        swapped = jnp.where(first_half, pltpu.roll(t, LANES - MLA_ROPE_DIM // 2, 1),
                            pltpu.roll(t, MLA_ROPE_DIM // 2, 1))
        return t * cos + swapped * sin

    scale = (MLA_NOPE_DIM + MLA_ROPE_DIM) ** -0.5
    k_rope = jnp.where(low, rope(misc), 0.0).astype(BF16)
    nope_w = N_HEADS * MLA_NOPE_DIM
    for h in range(N_HEADS):
        pair = rope(mq[:, nope_w + (h // 2) * LANES: nope_w + (h // 2 + 1) * LANES])
        q_rope = pair if h % 2 == 0 else pltpu.roll(pair, MLA_ROPE_DIM, 1)
        q_rope = jnp.where(low, q_rope, 0.0)
        base = h * _MLA_QK_PAD
        q_ref[:, base:base + HEAD_DIM] = (mq[:, h * HEAD_DIM:(h + 1) * HEAD_DIM] * scale).astype(BF16)
        q_ref[:, base + HEAD_DIM:base + _MLA_QK_PAD] = (q_rope * scale).astype(BF16)
        k_ref[:, base:base + HEAD_DIM] = mkv[:, h * HEAD_DIM:(h + 1) * HEAD_DIM].astype(BF16)
        k_ref[:, base + HEAD_DIM:base + _MLA_QK_PAD] = k_rope
    v_ref[...] = mkv[:, nope_w:].astype(BF16)


def _mla_prep(proj, q_norm_w, kv_norm_w, w_uq, w_ukv, cos, sin, *, seq, tm=512):
    t_all = proj.shape[0]
    tm = min(tm, seq)
    nblk = seq // tm
    full = lambda a: pl.BlockSpec(a.shape, lambda i: (0,) * a.ndim)
    table = pl.BlockSpec((tm, LANES), lambda i: (i % nblk, 0))
    qnw = q_norm_w.reshape(1, -1)
    kvnw = kv_norm_w.reshape(1, -1)
    wide = N_HEADS * _MLA_QK_PAD
    return pl.pallas_call(
        _mla_prep_kernel,
        grid=(t_all // tm,),
        in_specs=[pl.BlockSpec((tm, _MLA_IN), lambda i: (i, OFF_MLA // _MLA_IN)),
                  full(qnw), full(kvnw), full(w_uq), full(w_ukv), table, table],
        out_specs=[pl.BlockSpec((tm, wide), lambda i: (i, 0)),
                   pl.BlockSpec((tm, wide), lambda i: (i, 0)),
                   pl.BlockSpec((tm, GROUP_WIDTH), lambda i: (i, 0))],
        out_shape=[jax.ShapeDtypeStruct((t_all, wide), BF16),
                   jax.ShapeDtypeStruct((t_all, wide), BF16),
                   jax.ShapeDtypeStruct((t_all, GROUP_WIDTH), BF16)],
        compiler_params=_cparams("parallel"),
        name="mla_prep",
    )(proj, qnw, kvnw, w_uq, w_ukv, cos, sin)


def _mla_attn_kernel(q_ref, k_ref, v_ref, o_ref, m_ref, l_ref, acc_ref):
    i = pl.program_id(2)
    j = pl.program_id(3)

    @pl.when(j == 0)
    def _():
        m_ref[...] = jnp.full_like(m_ref, -jnp.inf)
        l_ref[...] = jnp.zeros_like(l_ref)
        acc_ref[...] = jnp.zeros_like(acc_ref)

    def step(diagonal):
        s = _dot_nt(q_ref[...], k_ref[...])
        if diagonal:
            row = lax.broadcasted_iota(jnp.int32, s.shape, 0)
            col = lax.broadcasted_iota(jnp.int32, s.shape, 1)
            s = jnp.where(row >= col, s, -jnp.inf)
        m_prev = m_ref[...]
        m_new = jnp.maximum(m_prev, jnp.max(s, axis=-1, keepdims=True))
        p = jnp.exp(s - m_new)
        corr = jnp.exp(m_prev - m_new)
        l_ref[...] = corr * l_ref[...] + jnp.sum(p, axis=-1, keepdims=True)
        acc_ref[...] = corr * acc_ref[...] + _dot(p.astype(BF16), v_ref[...])
        m_ref[...] = m_new

    @pl.when(j < i)
    def _():
        step(False)

    @pl.when(j == i)
    def _():
        step(True)
        o_ref[...] = (acc_ref[...] / l_ref[...]).astype(o_ref.dtype)


def _mla_attn(q, k, v, *, batch, seq, t=512):
    t = min(t, seq)
    nblk = seq // t
    kv_row = lambda b, h, i, j: b * nblk + jnp.minimum(j, i)
    return pl.pallas_call(
        _mla_attn_kernel,
        grid=(batch, N_HEADS, nblk, nblk),
        in_specs=[pl.BlockSpec((t, _MLA_QK_PAD), lambda b, h, i, j: (b * nblk + i, h)),
                  pl.BlockSpec((t, _MLA_QK_PAD), lambda b, h, i, j: (kv_row(b, h, i, j), h)),
                  pl.BlockSpec((t, HEAD_DIM), lambda b, h, i, j: (kv_row(b, h, i, j), h))],
        out_specs=pl.BlockSpec((t, HEAD_DIM), lambda b, h, i, j: (b * nblk + i, h)),
        out_shape=jax.ShapeDtypeStruct((batch * seq, GROUP_WIDTH), BF16),
        scratch_shapes=[pltpu.VMEM((t, 1), F32), pltpu.VMEM((t, 1), F32), pltpu.VMEM((t, HEAD_DIM), F32)],
        compiler_params=_cparams("parallel", "parallel", "parallel", "arbitrary"),
        name="mla_attn",
    )(q, k, v)


def _sgu_kernel(x_ref, nw_ref, nb_ref, ws_ref, bs_ref, o_ref, *, n_chunks):
    x = x_ref[...].astype(F32)
    ge = 0.5 * x * (1.0 + lax.erf(x * np.float32(math.sqrt(0.5))))
    u = ge[:, :GROUP_WIDTH]
    sv = ge[:, GROUP_WIDTH:]
    mu = jnp.mean(sv, axis=-1, keepdims=True)
    svc = sv - mu
    var = jnp.mean(svc * svc, axis=-1, keepdims=True)
    svn = (svc * lax.rsqrt(var + 1e-5) * nw_ref[...] + nb_ref[...]).astype(BF16)
    r_io = lax.broadcasted_iota(jnp.int32, (SGU_CHUNK, SGU_CHUNK), 0)
    c_io = lax.broadcasted_iota(jnp.int32, (SGU_CHUNK, SGU_CHUNK), 1)
    causal = r_io >= c_io
    gd = GROUP_WIDTH // SGU_GROUPS
    for g in range(SGU_GROUPS):
        w_mix = jnp.where(causal, ws_ref[g], 0.0).astype(BF16)
        bias = bs_ref[:, g:g + 1]
        cols = slice(g * gd, (g + 1) * gd)
        for c in range(n_chunks):
            rows = slice(c * SGU_CHUNK, (c + 1) * SGU_CHUNK)
            mixed = _dot(w_mix, svn[rows, cols]) + bias
            o_ref[rows, cols] = (u[rows, cols] * mixed).astype(BF16)


def _sgu(proj, norm_w, norm_b, w_s, b_s, *, rows=256):
    t_all = proj.shape[0]
    rows = min(rows, t_all)
    width = 2 * GROUP_WIDTH
    bs_t = jnp.pad(b_s.T, ((0, 0), (0, LANES - SGU_GROUPS)))
    full = lambda a: pl.BlockSpec(a.shape, lambda i: (0,) * a.ndim)
    nw = norm_w.reshape(1, -1)
    nb = norm_b.reshape(1, -1)
    return pl.pallas_call(
        functools.partial(_sgu_kernel, n_chunks=rows // SGU_CHUNK),
        grid=(t_all // rows,),
        in_specs=[pl.BlockSpec((rows, width), lambda i: (i, OFF_SGU // width)),
                  full(nw), full(nb), full(w_s), full(bs_t)],
        out_specs=pl.BlockSpec((rows, GROUP_WIDTH), lambda i: (i, 0)),
        out_shape=jax.ShapeDtypeStruct((t_all, GROUP_WIDTH), BF16),
        compiler_params=_cparams("parallel"),
        name="sgu",
    )(proj, nw, nb, w_s, bs_t)


def _sb_kernel(q_ref, k_ref, v_ref, o_ref, carry_ref, acc_ref):
    i = pl.program_id(2)
    j = pl.program_id(3)

    @pl.when(j == 0)
    def _():
        carry_ref[...] = jnp.zeros_like(carry_ref)
        acc_ref[...] = jnp.zeros_like(acc_ref)

    def step(diagonal):
        z = _dot_nt(q_ref[...], k_ref[...]) * np.float32(HEAD_DIM ** -0.5)
        row = lax.broadcasted_iota(jnp.int32, z.shape, 0)
        col = lax.broadcasted_iota(jnp.int32, z.shape, 1)
        log_beta = jnp.minimum(z, 0.0) - jnp.log1p(jnp.exp(-jnp.abs(z)))
        log_rest = log_beta - z
        if diagonal:
            log_rest = jnp.where(col < row, log_rest, 0.0)
        suffix = (row > col).astype(BF16)
        hi = log_rest.astype(BF16)
        lo = (log_rest - hi.astype(F32)).astype(BF16)
        tail = _dot(hi, suffix) + _dot(lo, suffix) + carry_ref[...]
        a = jnp.exp(log_beta + tail)
        if diagonal:
            a = jnp.where(col < row, a, 0.0)
        acc_ref[...] += _dot(a.astype(BF16), v_ref[...])
        carry_ref[...] += jnp.sum(log_rest, axis=-1, keepdims=True)

    @pl.when(j == 0)
    def _():
        step(True)

    @pl.when((j > 0) & (j <= i))
    def _():
        step(False)

    @pl.when(j == i)
    def _():
        o_ref[...] = acc_ref[...].astype(o_ref.dtype)


def _sb_attn(proj, *, batch, seq, t=512):
    t = min(t, seq)
    nblk = seq // t
    c0 = OFF_SB // HEAD_DIM
    kv_row = lambda b, h, i, j: b * nblk + jnp.maximum(i - j, 0)
    return pl.pallas_call(
        _sb_kernel,
        grid=(batch, N_HEADS, nblk, nblk),
        in_specs=[pl.BlockSpec((t, HEAD_DIM), lambda b, h, i, j: (b * nblk + i, c0 + h)),
                  pl.BlockSpec((t, HEAD_DIM), lambda b, h, i, j: (kv_row(b, h, i, j), c0 + N_HEADS + h)),
                  pl.BlockSpec((t, HEAD_DIM), lambda b, h, i, j: (kv_row(b, h, i, j), c0 + 2 * N_HEADS + h))],
        out_specs=pl.BlockSpec((t, HEAD_DIM), lambda b, h, i, j: (b * nblk + i, h)),
        out_shape=jax.ShapeDtypeStruct((batch * seq, GROUP_WIDTH), BF16),
        scratch_shapes=[pltpu.VMEM((t, 1), F32), pltpu.VMEM((t, HEAD_DIM), F32)],
        compiler_params=_cparams("parallel", "parallel", "parallel", "arbitrary"),
        name="sb_attn",
    )(proj, proj, proj)


def _reorder_w_in(w):
    wb = w.astype(BF16)
    seg = lambda a, n: wb[:, a:a + n]
    gdn_qkvz = seg(0, 4096)
    gdn_a, gdn_b = seg(4096, 8), seg(4104, 8)
    mla_cq, mla_ckv, mla_kr = seg(4112, MLA_Q_RANK), seg(5008, MLA_KV_RANK), seg(5520, MLA_ROPE_DIM)
    sgu, sb = seg(5584, 2048), seg(7632, 3072)
    pad = jnp.zeros((w.shape[0], LANES - MLA_ROPE_DIM - 2 * N_HEADS), BF16)
    return jnp.concatenate([gdn_qkvz, sgu, sb, mla_cq, mla_kr, gdn_a, gdn_b, pad, mla_ckv], axis=1)


def _split_heads_cols(w, first):
    k = w.shape[0]
    w3 = w.reshape(k, N_HEADS, -1)
    return jnp.concatenate([w3[:, :, :first].reshape(k, -1), w3[:, :, first:].reshape(k, -1)], axis=1)


def _rope_tables(seq):
    half = MLA_ROPE_DIM // 2
    pos = jnp.arange(seq, dtype=F32)
    inv_freq = ROPE_THETA ** (-jnp.arange(half, dtype=F32) / half)
    ang = pos[:, None] * inv_freq[None, :]
    cos, sin = jnp.cos(ang), jnp.sin(ang)
    reps = LANES // half
    sign = jnp.tile(jnp.concatenate([-jnp.ones((half,), F32), jnp.ones((half,), F32)]), LANES // MLA_ROPE_DIM)
    return jnp.tile(cos, (1, reps)), jnp.tile(sin, (1, reps)) * sign[None, :]


def kernel(x, w_in, gdn_conv_w, gdn_a_log, gdn_dt_bias, gdn_norm_w, mla_q_norm_w, mla_w_uq, mla_kv_norm_w, mla_w_ukv, sgu_norm_w, sgu_norm_b, sgu_w_s, sgu_b_s, w_out, ln_mix_w, ln_mix_b, ffn_w_gate, ffn_w_up, ffn_w_down, moe_router_w, moe_router_b, moe_w_gate, moe_w_up, moe_w_down, ln_ffn_w, ln_ffn_b):
    batch, seq, d = x.shape
    t_all = batch * seq
    cos, sin = _rope_tables(seq)
    xf = x.reshape(t_all, d)
    xb = xf.astype(BF16)
    ff_pad = D_FF_PAD - D_FF
    for layer in range(DEPTH):
        proj = _matmul([xb], _reorder_w_in(w_in[layer]), tm=1024, tn=768, tk=d, out_dtype=BF16, name="in_proj")
        qkv = _gdn_prep(proj, gdn_conv_w[layer], batch=batch, seq=seq)
        o_a = _gdn(qkv, proj, gdn_a_log[layer], gdn_dt_bias[layer], gdn_norm_w[layer], batch=batch, seq=seq)
        mq, mk, mv = _mla_prep(proj, mla_q_norm_w[layer], mla_kv_norm_w[layer],
                               _split_heads_cols(mla_w_uq[layer], MLA_NOPE_DIM).astype(BF16),
                               _split_heads_cols(mla_w_ukv[layer], MLA_NOPE_DIM).astype(BF16),
                               cos, sin, seq=seq)
        o_b = _mla_attn(mq, mk, mv, batch=batch, seq=seq)
        o_c = _sgu(proj, sgu_norm_w[layer], sgu_norm_b[layer], sgu_w_s[layer], sgu_b_s[layer])
        o_d = _sb_attn(proj, batch=batch, seq=seq)
        pre = _matmul([o_a, o_b, o_c, o_d], w_out[layer].astype(BF16), tm=1024, tn=512, tk=GROUP_WIDTH,
                      out_dtype=F32, res=xf, alpha=DEEPNORM_ALPHA, name="out_proj")
        xf, xb = _layer_norm(pre, ln_mix_w[layer], ln_mix_b[layer], name="ln_mix")
        i = layer // 2
        if layer % 2 == 0:
            wg = jnp.pad(ffn_w_gate[i].astype(BF16), ((0, 0), (0, ff_pad)))[None]
            wu = jnp.pad(ffn_w_up[i].astype(BF16), ((0, 0), (0, ff_pad)))[None]
            wd = jnp.pad(ffn_w_down[i].astype(BF16), ((0, ff_pad), (0, 0)))
            hid = _gate_up(xb, wg, wu, tm=1024, tn=512, name="ffn_gate_up")
            pre = _matmul([hid], wd, tm=1024, tn=1024, tk=2816, out_dtype=F32, res=xf, alpha=DEEPNORM_ALPHA,
                          name="ffn_down")
        else:
            gates = _router(xf, moe_router_w[i], moe_router_b[i])
            hid = _gate_up(xb, moe_w_gate[i].astype(BF16), moe_w_up[i].astype(BF16), tm=1024, tn=256,
                           gates=gates, name="moe_gate_up")
            wd = moe_w_down[i].astype(BF16).reshape(N_EXPERTS * D_EXPERT, d)
            pre = _matmul([hid], wd, tm=1024, tn=1024, tk=D_EXPERT, out_dtype=F32, res=xf, alpha=DEEPNORM_ALPHA,
                          name="moe_down")
        xf, xb = _layer_norm(pre, ln_ffn_w[layer], ln_ffn_b[layer], name="ln_ffn")
    return xf.reshape(batch, seq, d)
```

```python
import functools
import math

import jax
import jax.numpy as jnp
import numpy as np
from jax import lax
from jax.experimental import pallas as pl
from jax.experimental.pallas import tpu as pltpu

F32 = jnp.float32
BF16 = jnp.bfloat16
HIGHEST = lax.Precision.HIGHEST

D_MODEL = 4096
DEPTH = 4
HEAD_DIM = 128
GROUP_WIDTH = 1024
N_HEADS = 8
GDN_CONV = 4
GDN_CHUNK = 64
MLA_Q_RANK = 896
MLA_KV_RANK = 512
MLA_NOPE_DIM = 128
MLA_ROPE_DIM = 64
ROPE_THETA = 10000.0
SGU_CHUNK = 128
SGU_GROUPS = 8
D_FF = 11008
N_EXPERTS = 8
D_EXPERT = 1792
DEEPNORM_ALPHA = (2 * DEPTH) ** 0.25

LANES = 128
SUBLANES = 8
VMEM_LIMIT_BYTES = 56 * 1024 * 1024

OFF_GDN_QKV = 0
OFF_GDN_Z = 3072
OFF_SGU = 4096
OFF_SB = 6144
OFF_MLA = 9216
OFF_MISC = OFF_MLA + MLA_Q_RANK
D_IN_PAD = 10752
MISC_A = 64
MISC_B = 72
D_FF_PAD = 11264


def _cparams(*sem):
    return pltpu.CompilerParams(dimension_semantics=sem, vmem_limit_bytes=VMEM_LIMIT_BYTES)


def _dot(a, b):
    return jnp.dot(a, b, preferred_element_type=F32)


def _dot_nt(a, b, precision=None):
    return lax.dot_general(a, b, (((1,), (1,)), ((), ())), preferred_element_type=F32, precision=precision)


def _dot_tn(a, b):
    return lax.dot_general(a, b, (((0,), (0,)), ((), ())), preferred_element_type=F32)


def _silu(x):
    return x * jax.nn.sigmoid(x)


def _mm_kernel(*refs, n_a, nk, alpha, has_res):
    a_refs = refs[:n_a]
    w_refs = refs[n_a:2 * n_a]
    pos = 2 * n_a
    res_ref = refs[pos] if has_res else None
    pos += int(has_res)
    o_ref = refs[pos]
    acc_ref = refs[pos + 1] if nk > 1 else None

    def partial_product():
        acc = _dot(a_refs[0][...], w_refs[0][...])
        for a_ref, w_ref in zip(a_refs[1:], w_refs[1:]):
            acc = acc + _dot(a_ref[...], w_ref[...])
        return acc

    def epilogue(acc):
        if has_res:
            acc = acc + alpha * res_ref[...]
        o_ref[...] = acc.astype(o_ref.dtype)

    if nk == 1:
        epilogue(partial_product())
    else:
        k = pl.program_id(2)

        @pl.when(k == 0)
        def _():
            acc_ref[...] = jnp.zeros_like(acc_ref)

        acc_ref[...] += partial_product()

        @pl.when(k == nk - 1)
        def _():
            epilogue(acc_ref[...])


def _matmul(a_list, w, *, tm, tn, tk, out_dtype, res=None, alpha=None, name="matmul"):
    n_a = len(a_list)
    m, k_seg = a_list[0].shape
    n = w.shape[-1]
    tm, tn, tk = min(tm, m), min(tn, n), min(tk, k_seg)
    nk = k_seg // tk
    assert m % tm == 0 and n % tn == 0 and k_seg % tk == 0
    assert n_a == 1 or nk == 1
    in_specs = [pl.BlockSpec((tm, tk), lambda i, j, k: (i, k)) for _ in a_list]
    in_specs += [pl.BlockSpec((tk, tn), functools.partial(lambda i, j, k, s: (s * nk + k, j), s=s))
                 for s in range(n_a)]
    operands = list(a_list) + [w] * n_a
    if res is not None:
        in_specs.append(pl.BlockSpec((tm, tn), lambda i, j, k: (i, j)))
        operands.append(res)
    return pl.pallas_call(
        functools.partial(_mm_kernel, n_a=n_a, nk=nk, alpha=alpha, has_res=res is not None),
        grid=(m // tm, n // tn, nk),
        in_specs=in_specs,
        out_specs=pl.BlockSpec((tm, tn), lambda i, j, k: (i, j)),
        out_shape=jax.ShapeDtypeStruct((m, n), out_dtype),
        scratch_shapes=[pltpu.VMEM((tm, tn), F32)] if nk > 1 else [],
        compiler_params=_cparams("parallel", "parallel", "arbitrary"),
        name=name,
    )(*operands)


def _gate_up_kernel(*refs, tiles_per_expert, has_gates):
    if has_gates:
        x_ref, wg_ref, wu_ref, gates_ref, o_ref = refs
    else:
        x_ref, wg_ref, wu_ref, o_ref = refs
    x = x_ref[...]
    h = _silu(_dot(x, wg_ref[...])) * _dot(x, wu_ref[...])
    if has_gates:
        e = pl.program_id(1) // tiles_per_expert
        gates = gates_ref[...]
        lane = lax.broadcasted_iota(jnp.int32, gates.shape, 1)
        h = h * jnp.sum(jnp.where(lane == e, gates, 0.0), axis=1, keepdims=True)
    o_ref[...] = h.astype(o_ref.dtype)


def _gate_up(x, wg, wu, *, tm, tn, gates=None, name="gate_up"):
    m, k = x.shape
    n_exp, _, n_e = wg.shape
    tm, tn = min(tm, m), min(tn, n_e)
    assert m % tm == 0 and n_e % tn == 0
    tpe = n_e // tn
    w_spec = pl.BlockSpec((None, k, tn), lambda i, j: (j // tpe, 0, j % tpe))
    in_specs = [pl.BlockSpec((tm, k), lambda i, j: (i, 0)), w_spec, w_spec]
    operands = [x, wg, wu]
    if gates is not None:
        in_specs.append(pl.BlockSpec((tm, LANES), lambda i, j: (i, 0)))
        operands.append(gates)
    return pl.pallas_call(
        functools.partial(_gate_up_kernel, tiles_per_expert=tpe, has_gates=gates is not None),
        grid=(m // tm, n_exp * tpe),
        in_specs=in_specs,
        out_specs=pl.BlockSpec((tm, tn), lambda i, j: (i, j)),
        out_shape=jax.ShapeDtypeStruct((m, n_exp * n_e), BF16),
        compiler_params=_cparams("parallel", "arbitrary"),
        name=name,
    )(*operands)


def _ln_kernel(x_ref, w_ref, b_ref, o_ref, ob_ref):
    x = x_ref[...]
    mu = jnp.mean(x, axis=-1, keepdims=True)
    xc = x - mu
    var = jnp.mean(xc * xc, axis=-1, keepdims=True)
    y = xc * lax.rsqrt(var + 1e-5) * w_ref[...] + b_ref[...]
    o_ref[...] = y
    ob_ref[...] = y.astype(BF16)


def _layer_norm(x, w, b, *, tm=256, name="layer_norm"):
    m, d = x.shape
    tm = min(tm, m)
    row = pl.BlockSpec((tm, d), lambda i: (i, 0))
    vec = pl.BlockSpec((1, d), lambda i: (0, 0))
    return pl.pallas_call(
        _ln_kernel,
        grid=(m // tm,),
        in_specs=[row, vec, vec],
        out_specs=[row, row],
        out_shape=[jax.ShapeDtypeStruct((m, d), F32), jax.ShapeDtypeStruct((m, d), BF16)],
        compiler_params=_cparams("parallel"),
        name=name,
    )(x, w.reshape(1, d), b.reshape(1, d))


def _router_kernel(x_ref, w_ref, b_ref, o_ref):
    logits = jnp.dot(x_ref[...], w_ref[...], preferred_element_type=F32, precision=HIGHEST) + b_ref[...]
    lane = lax.broadcasted_iota(jnp.int32, logits.shape, 1)
    neg = jnp.float32(-jnp.inf)
    logits = jnp.where(lane < N_EXPERTS, logits, neg)
    m1 = jnp.max(logits, axis=1, keepdims=True)
    i1 = jnp.min(jnp.where(logits == m1, lane, LANES), axis=1, keepdims=True)
    rest = jnp.where(lane == i1, neg, logits)
    m2 = jnp.max(rest, axis=1, keepdims=True)
    i2 = jnp.min(jnp.where(rest == m2, lane, LANES), axis=1, keepdims=True)
    e2 = jnp.exp(m2 - m1)
    denom = 1.0 + e2
    o_ref[...] = jnp.where(lane == i1, 1.0 / denom, 0.0) + jnp.where(lane == i2, e2 / denom, 0.0)


def _router(x, w, b, *, tm=512, name="router"):
    m, d = x.shape
    tm = min(tm, m)
    w_pad = jnp.pad(w, ((0, 0), (0, LANES - N_EXPERTS)))
    b_pad = jnp.pad(b, (0, LANES - N_EXPERTS)).reshape(1, LANES)
    return pl.pallas_call(
        _router_kernel,
        grid=(m // tm,),
        in_specs=[pl.BlockSpec((tm, d), lambda i: (i, 0)),
                  pl.BlockSpec((d, LANES), lambda i: (0, 0)),
                  pl.BlockSpec((1, LANES), lambda i: (0, 0))],
        out_specs=pl.BlockSpec((tm, LANES), lambda i: (i, 0)),
        out_shape=jax.ShapeDtypeStruct((m, LANES), F32),
        compiler_params=_cparams("parallel"),
        name=name,
    )(x, w_pad, b_pad)


_CONV_HALO = SUBLANES


def _gdn_prep_kernel(x_ref, w_ref, o_ref, buf_ref, *, rows):
    @pl.when(pl.program_id(1) == 0)
    def _():
        buf_ref[0:_CONV_HALO, :] = jnp.zeros((_CONV_HALO, buf_ref.shape[1]), F32)

    buf_ref[_CONV_HALO:_CONV_HALO + rows, :] = x_ref[...].astype(F32)
    n_strips = x_ref.shape[1] // HEAD_DIM
    for s in range(n_strips):
        cols = slice(s * HEAD_DIM, (s + 1) * HEAD_DIM)
        acc = w_ref[GDN_CONV - 1:GDN_CONV, cols] * buf_ref[_CONV_HALO:_CONV_HALO + rows, cols]
        for i in range(GDN_CONV - 1):
            start = _CONV_HALO - (GDN_CONV - 1) + i
            acc = acc + w_ref[i:i + 1, cols] * buf_ref[start:start + rows, cols]
        y = _silu(acc)
        if s < 2 * N_HEADS:
            y = y * lax.rsqrt(jnp.sum(y * y, axis=-1, keepdims=True) + 1e-6)
            if s < N_HEADS:
                y = y * HEAD_DIM ** -0.5
        o_ref[:, cols] = y
    buf_ref[0:_CONV_HALO, :] = buf_ref[rows:rows + _CONV_HALO, :]


def _gdn_prep(proj, conv_w, *, batch, seq, rows=256):
    rows = min(rows, seq)
    width = 3 * GROUP_WIDTH
    nblk = seq // rows
    return pl.pallas_call(
        functools.partial(_gdn_prep_kernel, rows=rows),
        grid=(batch, nblk),
        in_specs=[pl.BlockSpec((rows, width), lambda b, s: (b * nblk + s, OFF_GDN_QKV // width)),
                  pl.BlockSpec((GDN_CONV, width), lambda b, s: (0, 0))],
        out_specs=pl.BlockSpec((rows, width), lambda b, s: (b * nblk + s, 0)),
        out_shape=jax.ShapeDtypeStruct((batch * seq, width), F32),
        scratch_shapes=[pltpu.VMEM((rows + _CONV_HALO, width), F32)],
        compiler_params=_cparams("parallel", "arbitrary"),
        name="gdn_prep",
    )(proj, conv_w)


def _split2(x):
    hi = x.astype(BF16)
    return hi, (x - hi.astype(F32)).astype(BF16)


def _split3(x):
    hi = x.astype(BF16)
    r = x - hi.astype(F32)
    mid = r.astype(BF16)
    return hi, mid, (r - mid.astype(F32)).astype(BF16)


def _dot_x3(a, b):
    a_hi, a_lo = _split2(a)
    b_hi, b_lo = _split2(b)
    return _dot(a_hi, b_hi) + (_dot(a_hi, b_lo) + _dot(a_lo, b_hi))


def _gdn_kernel(q_ref, k_ref, v_ref, z_ref, misc_ref, alog_ref, dtb_ref, nw_ref, o_ref, s_ref, *, n_chunks, hg):
    c_len = GDN_CHUNK
    h0 = pl.program_id(1) * hg

    @pl.when(pl.program_id(2) == 0)
    def _():
        s_ref[...] = jnp.zeros_like(s_ref)

    r_io = lax.broadcasted_iota(jnp.int32, (c_len, c_len), 0)
    c_io = lax.broadcasted_iota(jnp.int32, (c_len, c_len), 1)
    incl = r_io >= c_io
    strict = r_io > c_io
    l_incl = incl.astype(BF16)
    eye = (r_io == c_io).astype(F32)
    sel = lax.broadcasted_iota(jnp.int32, (LANES, LANES), 0)
    lane0 = (lax.broadcasted_iota(jnp.int32, (c_len, LANES), 1) == 0).astype(BF16)
    heads = range(hg)
    chunks = range(n_chunks)
    pairs = [(hh, c) for hh in heads for c in chunks]
    rows = [pl.ds(c * c_len, c_len) for c in chunks]
    cols = [slice(hh * HEAD_DIM, (hh + 1) * HEAD_DIM) for hh in heads]
    pick_a = [(sel == MISC_A + h0 + hh).astype(BF16) for hh in heads]
    pick_b = [(sel == MISC_B + h0 + hh).astype(BF16) for hh in heads]
    neg_a = [-jnp.exp(alog_ref[hh]) for hh in heads]
    misc = [misc_ref[rows[c], :] for c in chunks]

    def softplus(t):
        return jnp.maximum(t, 0.0) + jnp.log1p(jnp.exp(-jnp.abs(t)))

    g_b = {p: neg_a[p[0]] * softplus(_dot(misc[p[1]], pick_a[p[0]]) + dtb_ref[p[0]]) for p in pairs}
    beta_b = {p: jax.nn.sigmoid(_dot(misc[p[1]], pick_b[p[0]])) for p in pairs}
    gc_b = {p: sum(_dot(l_incl, part) for part in _split3(g_b[p])) for p in pairs}
    gc_row = {p: sum(_dot_nt(lane0, part) for part in _split3(gc_b[p])) for p in pairs}
    decay = {p: jnp.where(incl, jnp.exp(gc_b[p][:, :c_len] - gc_row[p]), 0.0) for p in pairs}

    q = {p: q_ref[rows[p[1]], cols[p[0]]] for p in pairs}
    k = {p: k_ref[rows[p[1]], cols[p[0]]] for p in pairs}
    k_bf = {p: k[p].astype(BF16) for p in pairs}
    kb = {p: k[p] * beta_b[p] for p in pairs}
    a_mat = {p: jnp.where(strict, _dot_nt(kb[p].astype(BF16), k_bf[p]) * decay[p], 0.0) for p in pairs}
    t_inv = {p: eye - a_mat[p] for p in pairs}
    a_pow = {p: _dot_x3(a_mat[p], a_mat[p]) for p in pairs}
    n_doublings = int(math.log2(c_len)) - 1
    for it in range(n_doublings):
        t_inv = {p: t_inv[p] + _dot_x3(t_inv[p], a_pow[p]) for p in pairs}
        if it < n_doublings - 1:
            a_pow = {p: _dot_x3(a_pow[p], a_pow[p]) for p in pairs}
    eg = {p: jnp.exp(gc_b[p]) for p in pairs}
    sol = {p: _dot_x3(t_inv[p], jnp.concatenate([kb[p] * eg[p], v_ref[rows[p[1]], cols[p[0]]] * beta_b[p]], axis=1))
           for p in pairs}
    w_bf = {p: sol[p][:, :HEAD_DIM].astype(BF16) for p in pairs}
    qk_bf = {p: (_dot_nt(q[p].astype(BF16), k_bf[p]) * decay[p]).astype(BF16) for p in pairs}
    qd_bf = {p: (q[p] * eg[p]).astype(BF16) for p in pairs}
    g_last = {p: gc_b[p][c_len - 1:c_len, :] for p in pairs}
    kd_bf = {p: (k[p] * jnp.exp(g_last[p] - gc_b[p])).astype(BF16) for p in pairs}
    eg_last = {p: jnp.exp(g_last[p]) for p in pairs}

    state = [s_ref[hh] for hh in heads]
    for c in chunks:
        state_bf = [state[hh].astype(BF16) for hh in heads]
        v_new = [sol[hh, c][:, HEAD_DIM:] - _dot(w_bf[hh, c], state_bf[hh]) for hh in heads]
        v_new_bf = [t.astype(BF16) for t in v_new]
        out = [_dot(qd_bf[hh, c], state_bf[hh]) + _dot(qk_bf[hh, c], v_new_bf[hh]) for hh in heads]
        state = [state[hh] * eg_last[hh, c] + _dot_tn(kd_bf[hh, c], v_new_bf[hh]) for hh in heads]
        for hh in heads:
            y = out[hh] * lax.rsqrt(jnp.mean(out[hh] * out[hh], axis=-1, keepdims=True) + 1e-6) * nw_ref[...]
            o_ref[rows[c], cols[hh]] = (y * _silu(z_ref[rows[c], cols[hh]].astype(F32))).astype(BF16)
    for hh in heads:
        s_ref[hh] = state[hh]


def _gdn(qkv, proj, a_log, dt_bias, norm_w, *, batch, seq, rows=256, hg=4):
    rows = min(rows, seq)
    nblk = seq // rows
    width = hg * HEAD_DIM
    groups = N_HEADS // hg

    def head_spec(col0):
        return pl.BlockSpec((rows, width), lambda b, g, s: (b * nblk + s, col0 // hg + g))

    per_head = pl.BlockSpec((hg, 1, LANES), lambda b, g, s: (g, 0, 0))
    bcast = lambda t: jnp.broadcast_to(t.astype(F32)[:, None, None], (N_HEADS, 1, LANES))
    return pl.pallas_call(
        functools.partial(_gdn_kernel, n_chunks=rows // GDN_CHUNK, hg=hg),
        grid=(batch, groups, nblk),
        in_specs=[head_spec(0), head_spec(N_HEADS), head_spec(2 * N_HEADS),
                  head_spec(OFF_GDN_Z // HEAD_DIM),
                  pl.BlockSpec((rows, LANES), lambda b, g, s: (b * nblk + s, OFF_MISC // LANES)),
                  per_head, per_head,
                  pl.BlockSpec((1, HEAD_DIM), lambda b, g, s: (0, 0))],
        out_specs=head_spec(0),
        out_shape=jax.ShapeDtypeStruct((batch * seq, GROUP_WIDTH), BF16),
        scratch_shapes=[pltpu.VMEM((hg, HEAD_DIM, HEAD_DIM), F32)],
        compiler_params=_cparams("parallel", "parallel", "arbitrary"),
        name="gdn",
    )(qkv, qkv, qkv, proj, proj, bcast(a_log), bcast(dt_bias), norm_w.reshape(1, HEAD_DIM))


_MLA_QK_PAD = 2 * HEAD_DIM
_MLA_IN = MLA_Q_RANK + LANES + MLA_KV_RANK


def _mla_prep_kernel(x_ref, qnw_ref, kvnw_ref, wuq_ref, wukv_ref, cos_ref, sin_ref, q_ref, k_ref, v_ref):
    x = x_ref[...]
    cq = x[:, :MLA_Q_RANK].astype(F32)
    misc = x[:, MLA_Q_RANK:MLA_Q_RANK + LANES].astype(F32)
    ckv = x[:, MLA_Q_RANK + LANES:].astype(F32)

    def rms(t, w):
        return t * lax.rsqrt(jnp.mean(t * t, axis=-1, keepdims=True) + 1e-6) * w

    mq = _dot(rms(cq, qnw_ref[...]).astype(BF16), wuq_ref[...])
    mkv = _dot(rms(ckv, kvnw_ref[...]).astype(BF16), wukv_ref[...])
    cos = cos_ref[...]
    sin = sin_ref[...]
    lane = lax.broadcasted_iota(jnp.int32, cos.shape, 1)
    first_half = (lane % MLA_ROPE_DIM) < MLA_ROPE_DIM // 2
    low = lane < MLA_ROPE_DIM

    def rope(t):
        swapped = jnp.where(first_half, pltpu.roll(t, LANES - MLA_ROPE_DIM // 2, 1),
                            pltpu.roll(t, MLA_ROPE_DIM // 2, 1))
        return t * cos + swapped * sin

    scale = (MLA_NOPE_DIM + MLA_ROPE_DIM) ** -0.5 * math.log2(math.e)
    k_rope = jnp.where(low, rope(misc), 0.0).astype(BF16)
    nope_w = N_HEADS * MLA_NOPE_DIM
    for h in range(N_HEADS):
        pair = rope(mq[:, nope_w + (h // 2) * LANES: nope_w + (h // 2 + 1) * LANES])
        q_rope = pair if h % 2 == 0 else pltpu.roll(pair, MLA_ROPE_DIM, 1)
        q_rope = jnp.where(low, q_rope, 0.0)
        base = h * _MLA_QK_PAD
        q_ref[:, base:base + HEAD_DIM] = (mq[:, h * HEAD_DIM:(h + 1) * HEAD_DIM] * scale).astype(BF16)
        q_ref[:, base + HEAD_DIM:base + _MLA_QK_PAD] = (q_rope * scale).astype(BF16)
        k_ref[:, base:base + HEAD_DIM] = mkv[:, h * HEAD_DIM:(h + 1) * HEAD_DIM].astype(BF16)
        k_ref[:, base + HEAD_DIM:base + _MLA_QK_PAD] = k_rope
    v_ref[...] = mkv[:, nope_w:].astype(BF16)


def _mla_prep(proj, q_norm_w, kv_norm_w, w_uq, w_ukv, cos, sin, *, seq, tm=512):
    t_all = proj.shape[0]
    tm = min(tm, seq)
    nblk = seq // tm
    full = lambda a: pl.BlockSpec(a.shape, lambda i: (0,) * a.ndim)
    table = pl.BlockSpec((tm, LANES), lambda i: (i % nblk, 0))
    qnw = q_norm_w.reshape(1, -1)
    kvnw = kv_norm_w.reshape(1, -1)
    wide = N_HEADS * _MLA_QK_PAD
    return pl.pallas_call(
        _mla_prep_kernel,
        grid=(t_all // tm,),
        in_specs=[pl.BlockSpec((tm, _MLA_IN), lambda i: (i, OFF_MLA // _MLA_IN)),
                  full(qnw), full(kvnw), full(w_uq), full(w_ukv), table, table],
        out_specs=[pl.BlockSpec((tm, wide), lambda i: (i, 0)),
                   pl.BlockSpec((tm, wide), lambda i: (i, 0)),
                   pl.BlockSpec((tm, GROUP_WIDTH), lambda i: (i, 0))],
        out_shape=[jax.ShapeDtypeStruct((t_all, wide), BF16),
                   jax.ShapeDtypeStruct((t_all, wide), BF16),
                   jax.ShapeDtypeStruct((t_all, GROUP_WIDTH), BF16)],
        compiler_params=_cparams("parallel"),
        name="mla_prep",
    )(proj, qnw, kvnw, w_uq, w_ukv, cos, sin)


def _mla_attn_kernel(q_ref, k_ref, v_ref, o_ref, m_ref, l_ref, acc_ref, *, t):
    i = pl.program_id(2)
    m_ref[...] = jnp.full_like(m_ref, -jnp.inf)
    l_ref[...] = jnp.zeros_like(l_ref)
    acc_ref[...] = jnp.zeros_like(acc_ref)

    def tile(kb, diagonal):
        start = pl.multiple_of(kb * t, t)
        s = _dot_nt(q_ref[...], k_ref[pl.ds(start, t), :])
        if diagonal:
            row = lax.broadcasted_iota(jnp.int32, s.shape, 0)
            col = lax.broadcasted_iota(jnp.int32, s.shape, 1)
            s = jnp.where(row >= col, s, -jnp.inf)
        m_prev = m_ref[...]
        m_new = jnp.maximum(m_prev, jnp.max(s, axis=-1, keepdims=True))
        p = jnp.exp2(s - m_new)
        corr = jnp.exp2(m_prev - m_new)
        l_ref[...] = corr * l_ref[...] + jnp.sum(p, axis=-1, keepdims=True)
        acc_ref[...] = corr * acc_ref[...] + _dot(p.astype(BF16), v_ref[pl.ds(start, t), :])
        m_ref[...] = m_new

    def body(kb, carry):
        tile(kb, False)
        return carry

    lax.fori_loop(0, i, body, 0)
    tile(i, True)
    o_ref[...] = (acc_ref[...] / l_ref[...]).astype(o_ref.dtype)


def _mla_attn(q, k, v, *, batch, seq, t=512):
    t = min(t, seq)
    nblk = seq // t
    return pl.pallas_call(
        functools.partial(_mla_attn_kernel, t=t),
        grid=(batch, N_HEADS, nblk),
        in_specs=[pl.BlockSpec((t, _MLA_QK_PAD), lambda b, h, i: (b * nblk + i, h)),
                  pl.BlockSpec((seq, _MLA_QK_PAD), lambda b, h, i: (b, h)),
                  pl.BlockSpec((seq, HEAD_DIM), lambda b, h, i: (b, h))],
        out_specs=pl.BlockSpec((t, HEAD_DIM), lambda b, h, i: (b * nblk + i, h)),
        out_shape=jax.ShapeDtypeStruct((batch * seq, GROUP_WIDTH), BF16),
        scratch_shapes=[pltpu.VMEM((t, 1), F32), pltpu.VMEM((t, 1), F32), pltpu.VMEM((t, HEAD_DIM), F32)],
        compiler_params=_cparams("parallel", "parallel", "arbitrary"),
        name="mla_attn",
    )(q, k, v)


def _sgu_kernel(x_ref, nw_ref, nb_ref, ws_ref, bs_ref, o_ref, *, n_chunks):
    x = x_ref[...].astype(F32)
    ge = 0.5 * x * (1.0 + lax.erf(x * np.float32(math.sqrt(0.5))))
    u = ge[:, :GROUP_WIDTH]
    sv = ge[:, GROUP_WIDTH:]
    mu = jnp.mean(sv, axis=-1, keepdims=True)
    svc = sv - mu
    var = jnp.mean(svc * svc, axis=-1, keepdims=True)
    svn = (svc * lax.rsqrt(var + 1e-5) * nw_ref[...] + nb_ref[...]).astype(BF16)
    r_io = lax.broadcasted_iota(jnp.int32, (SGU_CHUNK, SGU_CHUNK), 0)
    c_io = lax.broadcasted_iota(jnp.int32, (SGU_CHUNK, SGU_CHUNK), 1)
    causal = r_io >= c_io
    gd = GROUP_WIDTH // SGU_GROUPS
    for g in range(SGU_GROUPS):
        w_mix = jnp.where(causal, ws_ref[g], 0.0).astype(BF16)
        bias = bs_ref[:, g:g + 1]
        cols = slice(g * gd, (g + 1) * gd)
        for c in range(n_chunks):
            rows = slice(c * SGU_CHUNK, (c + 1) * SGU_CHUNK)
            mixed = _dot(w_mix, svn[rows, cols]) + bias
            o_ref[rows, cols] = (u[rows, cols] * mixed).astype(BF16)


def _sgu(proj, norm_w, norm_b, w_s, b_s, *, rows=256):
    t_all = proj.shape[0]
    rows = min(rows, t_all)
    width = 2 * GROUP_WIDTH
    bs_t = jnp.pad(b_s.T, ((0, 0), (0, LANES - SGU_GROUPS)))
    full = lambda a: pl.BlockSpec(a.shape, lambda i: (0,) * a.ndim)
    nw = norm_w.reshape(1, -1)
    nb = norm_b.reshape(1, -1)
    return pl.pallas_call(
        functools.partial(_sgu_kernel, n_chunks=rows // SGU_CHUNK),
        grid=(t_all // rows,),
        in_specs=[pl.BlockSpec((rows, width), lambda i: (i, OFF_SGU // width)),
                  full(nw), full(nb), full(w_s), full(bs_t)],
        out_specs=pl.BlockSpec((rows, GROUP_WIDTH), lambda i: (i, 0)),
        out_shape=jax.ShapeDtypeStruct((t_all, GROUP_WIDTH), BF16),
        compiler_params=_cparams("parallel"),
        name="sgu",
    )(proj, nw, nb, w_s, bs_t)


_SB_SUB = 256


def _sb_kernel(q_ref, k_ref, v_ref, o_ref, carry_ref, acc_ref, *, t):
    i = pl.program_id(2)
    sub = min(_SB_SUB, t)
    carry_ref[...] = jnp.zeros_like(carry_ref)
    acc_ref[...] = jnp.zeros_like(acc_ref)
    sr = lax.broadcasted_iota(jnp.int32, (2 * sub, sub), 0)
    sc = lax.broadcasted_iota(jnp.int32, (2 * sub, sub), 1)
    suffix2 = (jnp.where(sr >= sub, sr - sub, sr) > sc).astype(BF16)

    def tile(kb, diagonal):
        start = pl.multiple_of(kb * t, t)
        z = _dot_nt(q_ref[...], k_ref[pl.ds(start, t), :]) * np.float32(HEAD_DIM ** -0.5)
        log_beta = jnp.minimum(z, 0.0) - jnp.log(1.0 + jnp.exp(-jnp.abs(z)))
        log_rest = log_beta - z
        if diagonal:
            strict = lax.broadcasted_iota(jnp.int32, z.shape, 1) < lax.broadcasted_iota(jnp.int32, z.shape, 0)
            log_rest = jnp.where(strict, log_rest, 0.0)
        carry = carry_ref[...]
        acc = acc_ref[...]
        for sb in reversed(range(t // sub)):
            cs = slice(sb * sub, (sb + 1) * sub)
            lr = log_rest[:, cs]
            hi, lo = _split2(lr)
            tail = _dot(jnp.concatenate([hi, lo], axis=1), suffix2)
            a = jnp.exp(log_beta[:, cs] + (tail + carry))
            if diagonal:
                a = jnp.where(strict[:, cs], a, 0.0)
            acc = acc + _dot(a.astype(BF16), v_ref[pl.ds(start + sb * sub, sub), :])
            carry = carry + (tail[:, 0:1] + lr[:, 0:1])
        carry_ref[...] = carry
        acc_ref[...] = acc

    tile(i, True)

    def body(jj, c):
        tile(i - 1 - jj, False)
        return c

    lax.fori_loop(0, i, body, 0)
    o_ref[...] = acc_ref[...].astype(o_ref.dtype)


def _sb_attn(proj, *, batch, seq, t=512):
    t = min(t, seq)
    nblk = seq // t
    c0 = OFF_SB // HEAD_DIM
    return pl.pallas_call(
        functools.partial(_sb_kernel, t=t),
        grid=(batch, N_HEADS, nblk),
        in_specs=[pl.BlockSpec((t, HEAD_DIM), lambda b, h, i: (b * nblk + i, c0 + h)),
                  pl.BlockSpec((seq, HEAD_DIM), lambda b, h, i: (b, c0 + N_HEADS + h)),
                  pl.BlockSpec((seq, HEAD_DIM), lambda b, h, i: (b, c0 + 2 * N_HEADS + h))],
        out_specs=pl.BlockSpec((t, HEAD_DIM), lambda b, h, i: (b * nblk + i, h)),
        out_shape=jax.ShapeDtypeStruct((batch * seq, GROUP_WIDTH), BF16),
        scratch_shapes=[pltpu.VMEM((t, 1), F32), pltpu.VMEM((t, HEAD_DIM), F32)],
        compiler_params=_cparams("parallel", "parallel", "arbitrary"),
        name="sb_attn",
    )(proj, proj, proj)


def _reorder_w_in(w):
    wb = w.astype(BF16)
    seg = lambda a, n: wb[:, a:a + n]
    gdn_qkvz = seg(0, 4096)
    gdn_a, gdn_b = seg(4096, 8), seg(4104, 8)
    mla_cq, mla_ckv, mla_kr = seg(4112, MLA_Q_RANK), seg(5008, MLA_KV_RANK), seg(5520, MLA_ROPE_DIM)
    sgu, sb = seg(5584, 2048), seg(7632, 3072)
    pad = jnp.zeros((w.shape[0], LANES - MLA_ROPE_DIM - 2 * N_HEADS), BF16)
    return jnp.concatenate([gdn_qkvz, sgu, sb, mla_cq, mla_kr, gdn_a, gdn_b, pad, mla_ckv], axis=1)


def _split_heads_cols(w, first):
    k = w.shape[0]
    w3 = w.reshape(k, N_HEADS, -1)
    return jnp.concatenate([w3[:, :, :first].reshape(k, -1), w3[:, :, first:].reshape(k, -1)], axis=1)


def _rope_tables(seq):
    half = MLA_ROPE_DIM // 2
    pos = jnp.arange(seq, dtype=F32)
    inv_freq = ROPE_THETA ** (-jnp.arange(half, dtype=F32) / half)
    ang = pos[:, None] * inv_freq[None, :]
    cos, sin = jnp.cos(ang), jnp.sin(ang)
    reps = LANES // half
    sign = jnp.tile(jnp.concatenate([-jnp.ones((half,), F32), jnp.ones((half,), F32)]), LANES // MLA_ROPE_DIM)
    return jnp.tile(cos, (1, reps)), jnp.tile(sin, (1, reps)) * sign[None, :]


def kernel(x, w_in, gdn_conv_w, gdn_a_log, gdn_dt_bias, gdn_norm_w, mla_q_norm_w, mla_w_uq, mla_kv_norm_w, mla_w_ukv, sgu_norm_w, sgu_norm_b, sgu_w_s, sgu_b_s, w_out, ln_mix_w, ln_mix_b, ffn_w_gate, ffn_w_up, ffn_w_down, moe_router_w, moe_router_b, moe_w_gate, moe_w_up, moe_w_down, ln_ffn_w, ln_ffn_b):
    batch, seq, d = x.shape
    t_all = batch * seq
    cos, sin = _rope_tables(seq)
    xf = x.reshape(t_all, d)
    xb = xf.astype(BF16)
    ff_pad = D_FF_PAD - D_FF
    for layer in range(DEPTH):
        proj = _matmul([xb], _reorder_w_in(w_in[layer]), tm=1024, tn=768, tk=d, out_dtype=BF16, name="in_proj")
        qkv = _gdn_prep(proj, gdn_conv_w[layer], batch=batch, seq=seq)
        o_a = _gdn(qkv, proj, gdn_a_log[layer], gdn_dt_bias[layer], gdn_norm_w[layer], batch=batch, seq=seq)
        mq, mk, mv = _mla_prep(proj, mla_q_norm_w[layer], mla_kv_norm_w[layer],
                               _split_heads_cols(mla_w_uq[layer], MLA_NOPE_DIM).astype(BF16),
                               _split_heads_cols(mla_w_ukv[layer], MLA_NOPE_DIM).astype(BF16),
                               cos, sin, seq=seq)
        o_b = _mla_attn(mq, mk, mv, batch=batch, seq=seq)
        o_c = _sgu(proj, sgu_norm_w[layer], sgu_norm_b[layer], sgu_w_s[layer], sgu_b_s[layer])
        o_d = _sb_attn(proj, batch=batch, seq=seq)
        pre = _matmul([o_a, o_b, o_c, o_d], w_out[layer].astype(BF16), tm=1024, tn=512, tk=GROUP_WIDTH,
                      out_dtype=F32, res=xf, alpha=DEEPNORM_ALPHA, name="out_proj")
        xf, xb = _layer_norm(pre, ln_mix_w[layer], ln_mix_b[layer], name="ln_mix")
        i = layer // 2
        if layer % 2 == 0:
            wg = jnp.pad(ffn_w_gate[i].astype(BF16), ((0, 0), (0, ff_pad)))[None]
            wu = jnp.pad(ffn_w_up[i].astype(BF16), ((0, 0), (0, ff_pad)))[None]
            wd = jnp.pad(ffn_w_down[i].astype(BF16), ((0, ff_pad), (0, 0)))
            hid = _gate_up(xb, wg, wu, tm=1024, tn=512, name="ffn_gate_up")
            pre = _matmul([hid], wd, tm=1024, tn=1024, tk=2816, out_dtype=F32, res=xf, alpha=DEEPNORM_ALPHA,
                          name="ffn_down")
        else:
            gates = _router(xf, moe_router_w[i], moe_router_b[i])
            hid = _gate_up(xb, moe_w_gate[i].astype(BF16), moe_w_up[i].astype(BF16), tm=1024, tn=256,
                           gates=gates, name="moe_gate_up")
            wd = moe_w_down[i].astype(BF16).reshape(N_EXPERTS * D_EXPERT, d)
            pre = _matmul([hid], wd, tm=1024, tn=1024, tk=D_EXPERT, out_dtype=F32, res=xf, alpha=DEEPNORM_ALPHA,
                          name="moe_down")
        xf, xb = _layer_norm(pre, ln_ffn_w[layer], ln_ffn_b[layer], name="ln_ffn")
    return xf.reshape(batch, seq, d)
```

```python
import functools
import math

import jax
import jax.numpy as jnp
import numpy as np
from jax import lax
from jax.experimental import pallas as pl
from jax.experimental.pallas import tpu as pltpu

F32 = jnp.float32
BF16 = jnp.bfloat16
HIGHEST = lax.Precision.HIGHEST

D_MODEL = 4096
DEPTH = 4
HEAD_DIM = 128
GROUP_WIDTH = 1024
N_HEADS = 8
GDN_CONV = 4
GDN_CHUNK = 64
MLA_Q_RANK = 896
MLA_KV_RANK = 512
MLA_NOPE_DIM = 128
MLA_ROPE_DIM = 64
ROPE_THETA = 10000.0
SGU_CHUNK = 128
SGU_GROUPS = 8
D_FF = 11008
N_EXPERTS = 8
D_EXPERT = 1792
DEEPNORM_ALPHA = (2 * DEPTH) ** 0.25

LANES = 128
SUBLANES = 8
VMEM_LIMIT_BYTES = 56 * 1024 * 1024

OFF_GDN_QKV = 0
OFF_GDN_Z = 3072
OFF_SGU = 4096
OFF_SB = 6144
OFF_MLA = 9216
OFF_MISC = OFF_MLA + MLA_Q_RANK
D_IN_PAD = 10752
MISC_A = 64
MISC_B = 72
D_FF_PAD = 11264


def _cparams(*sem):
    return pltpu.CompilerParams(dimension_semantics=sem, vmem_limit_bytes=VMEM_LIMIT_BYTES)


def _dot(a, b):
    return jnp.dot(a, b, preferred_element_type=F32)


def _dot_nt(a, b, precision=None):
    return lax.dot_general(a, b, (((1,), (1,)), ((), ())), preferred_element_type=F32, precision=precision)


def _dot_tn(a, b):
    return lax.dot_general(a, b, (((0,), (0,)), ((), ())), preferred_element_type=F32)


def _silu(x):
    return x * jax.nn.sigmoid(x)


def _mm_kernel(*refs, n_a, nk, alpha, has_res):
    a_refs = refs[:n_a]
    w_refs = refs[n_a:2 * n_a]
    pos = 2 * n_a
    res_ref = refs[pos] if has_res else None
    pos += int(has_res)
    o_ref = refs[pos]
    acc_ref = refs[pos + 1] if nk > 1 else None

    def partial_product():
        acc = _dot(a_refs[0][...], w_refs[0][...])
        for a_ref, w_ref in zip(a_refs[1:], w_refs[1:]):
            acc = acc + _dot(a_ref[...], w_ref[...])
        return acc

    def epilogue(acc):
        if has_res:
            acc = acc + alpha * res_ref[...]
        o_ref[...] = acc.astype(o_ref.dtype)

    if nk == 1:
        epilogue(partial_product())
    else:
        k = pl.program_id(2)

        @pl.when(k == 0)
        def _():
            acc_ref[...] = jnp.zeros_like(acc_ref)

        acc_ref[...] += partial_product()

        @pl.when(k == nk - 1)
        def _():
            epilogue(acc_ref[...])


def _matmul(a_list, w, *, tm, tn, tk, out_dtype, res=None, alpha=None, name="matmul"):
    n_a = len(a_list)
    m, k_seg = a_list[0].shape
    n = w.shape[-1]
    tm, tn, tk = min(tm, m), min(tn, n), min(tk, k_seg)
    nk = k_seg // tk
    assert m % tm == 0 and n % tn == 0 and k_seg % tk == 0
    assert n_a == 1 or nk == 1
    in_specs = [pl.BlockSpec((tm, tk), lambda i, j, k: (i, k)) for _ in a_list]
    in_specs += [pl.BlockSpec((tk, tn), functools.partial(lambda i, j, k, s: (s * nk + k, j), s=s))
                 for s in range(n_a)]
    operands = list(a_list) + [w] * n_a
    if res is not None:
        in_specs.append(pl.BlockSpec((tm, tn), lambda i, j, k: (i, j)))
        operands.append(res)
    return pl.pallas_call(
        functools.partial(_mm_kernel, n_a=n_a, nk=nk, alpha=alpha, has_res=res is not None),
        grid=(m // tm, n // tn, nk),
        in_specs=in_specs,
        out_specs=pl.BlockSpec((tm, tn), lambda i, j, k: (i, j)),
        out_shape=jax.ShapeDtypeStruct((m, n), out_dtype),
        scratch_shapes=[pltpu.VMEM((tm, tn), F32)] if nk > 1 else [],
        compiler_params=_cparams("parallel", "parallel", "arbitrary"),
        name=name,
    )(*operands)


def _gate_up_kernel(x_ref, wg_ref, wu_ref, o_ref):
    x = x_ref[...]
    o_ref[...] = (_silu(_dot(x, wg_ref[...])) * _dot(x, wu_ref[...])).astype(o_ref.dtype)


def _gate_up(x, wg, wu, *, tm, tn, name="gate_up"):
    m, k = x.shape
    n = wg.shape[-1]
    tm, tn = min(tm, m), min(tn, n)
    assert m % tm == 0 and n % tn == 0
    w_spec = pl.BlockSpec((k, tn), lambda i, j: (0, j))
    return pl.pallas_call(
        _gate_up_kernel,
        grid=(m // tm, n // tn),
        in_specs=[pl.BlockSpec((tm, k), lambda i, j: (i, 0)), w_spec, w_spec],
        out_specs=pl.BlockSpec((tm, tn), lambda i, j: (i, j)),
        out_shape=jax.ShapeDtypeStruct((m, n), BF16),
        compiler_params=_cparams("parallel", "arbitrary"),
        name=name,
    )(x, wg, wu)


def _ln_kernel(x_ref, w_ref, b_ref, o_ref, ob_ref):
    x = x_ref[...]
    mu = jnp.mean(x, axis=-1, keepdims=True)
    xc = x - mu
    var = jnp.mean(xc * xc, axis=-1, keepdims=True)
    y = xc * lax.rsqrt(var + 1e-5) * w_ref[...] + b_ref[...]
    o_ref[...] = y
    ob_ref[...] = y.astype(BF16)


def _layer_norm(x, w, b, *, tm=256, name="layer_norm"):
    m, d = x.shape
    tm = min(tm, m)
    row = pl.BlockSpec((tm, d), lambda i: (i, 0))
    vec = pl.BlockSpec((1, d), lambda i: (0, 0))
    return pl.pallas_call(
        _ln_kernel,
        grid=(m // tm,),
        in_specs=[row, vec, vec],
        out_specs=[row, row],
        out_shape=[jax.ShapeDtypeStruct((m, d), F32), jax.ShapeDtypeStruct((m, d), BF16)],
        compiler_params=_cparams("parallel"),
        name=name,
    )(x, w.reshape(1, d), b.reshape(1, d))


META_E1, META_E2, META_W1, META_W2, META_R1, META_R2 = range(6)


def _router_kernel(x_ref, w_ref, b_ref, meta_ref, counts_ref, cnt_ref):
    @pl.when(pl.program_id(0) == 0)
    def _():
        cnt_ref[...] = jnp.zeros_like(cnt_ref)

    logits = jnp.dot(x_ref[...], w_ref[...], preferred_element_type=F32, precision=HIGHEST) + b_ref[...]
    tm = logits.shape[0]
    lane = lax.broadcasted_iota(jnp.int32, logits.shape, 1)
    neg = jnp.float32(-jnp.inf)
    logits = jnp.where(lane < N_EXPERTS, logits, neg)
    m1 = jnp.max(logits, axis=1, keepdims=True)
    i1 = jnp.min(jnp.where(logits == m1, lane, LANES), axis=1, keepdims=True)
    rest = jnp.where(lane == i1, neg, logits)
    m2 = jnp.max(rest, axis=1, keepdims=True)
    i2 = jnp.min(jnp.where(rest == m2, lane, LANES), axis=1, keepdims=True)
    e2 = jnp.exp(m2 - m1)
    denom = 1.0 + e2
    chosen = (lane == i1) | (lane == i2)
    earlier = (lax.broadcasted_iota(jnp.int32, (tm, tm), 0) > lax.broadcasted_iota(jnp.int32, (tm, tm), 1))
    before = _dot(earlier.astype(BF16), chosen.astype(BF16)) + cnt_ref[...]
    rank1 = jnp.sum(jnp.where(lane == i1, before, 0.0), axis=1, keepdims=True)
    rank2 = jnp.sum(jnp.where(lane == i2, before, 0.0), axis=1, keepdims=True)
    cnt_ref[...] += jnp.sum(chosen.astype(F32), axis=0, keepdims=True)
    fields = {META_E1: i1.astype(F32), META_E2: i2.astype(F32), META_W1: 1.0 / denom, META_W2: e2 / denom,
              META_R1: rank1, META_R2: rank2}
    meta = jnp.zeros(logits.shape, F32)
    for pos, val in fields.items():
        meta = jnp.where(lane == pos, val, meta)
    meta_ref[...] = meta
    counts_ref[...] = cnt_ref[...]


def _router(x, w, b, *, tm=512, name="router"):
    m, d = x.shape
    tm = min(tm, m)
    w_pad = jnp.pad(w, ((0, 0), (0, LANES - N_EXPERTS)))
    b_pad = jnp.pad(b, (0, LANES - N_EXPERTS)).reshape(1, LANES)
    return pl.pallas_call(
        _router_kernel,
        grid=(m // tm,),
        in_specs=[pl.BlockSpec((tm, d), lambda i: (i, 0)),
                  pl.BlockSpec((d, LANES), lambda i: (0, 0)),
                  pl.BlockSpec((1, LANES), lambda i: (0, 0))],
        out_specs=[pl.BlockSpec((tm, LANES), lambda i: (i, 0)),
                   pl.BlockSpec((1, LANES), lambda i: (0, 0))],
        out_shape=[jax.ShapeDtypeStruct((m, LANES), F32), jax.ShapeDtypeStruct((1, LANES), F32)],
        scratch_shapes=[pltpu.VMEM((1, LANES), F32)],
        compiler_params=_cparams("arbitrary"),
        name=name,
    )(x, w_pad, b_pad)


MOE_TILE = 512


def _moe_plan(meta, counts, t_all):
    as_int = lambda lane: meta[:, lane].astype(jnp.int32)
    e1, e2, r1, r2 = as_int(META_E1), as_int(META_E2), as_int(META_R1), as_int(META_R2)
    cnt = counts[0, :N_EXPERTS].astype(jnp.int32)
    padded = (cnt + MOE_TILE - 1) // MOE_TILE * MOE_TILE
    ends = jnp.cumsum(padded)
    starts = ends - padded
    pos1 = starts[e1] + r1
    pos2 = starts[e2] + r2
    n_rows = 2 * t_all + N_EXPERTS * MOE_TILE
    tok = jnp.arange(t_all, dtype=jnp.int32)
    src = jnp.zeros((n_rows,), jnp.int32).at[pos1].set(tok).at[pos2].set(tok)
    wrow = jnp.zeros((n_rows,), F32).at[pos1].set(meta[:, META_W1]).at[pos2].set(meta[:, META_W2])
    tile_start = jnp.arange(n_rows // MOE_TILE, dtype=jnp.int32) * MOE_TILE
    tile_expert = jnp.minimum(jnp.searchsorted(ends, tile_start, side="right"), N_EXPERTS - 1).astype(jnp.int32)
    return pos1, pos2, src, wrow, tile_expert


def _row_copy(src_hbm, row, dst_vmem, slot, sem):
    return pltpu.make_async_copy(src_hbm.at[pl.ds(row, 1), :], dst_vmem.at[pl.ds(slot, 1), :], sem)


def _moe_gather_kernel(src_ref, x_hbm, o_ref, buf_ref, sem, *, rows):
    def issue(r, c):
        _row_copy(x_hbm, src_ref[0, r], buf_ref, r, sem).start()
        return c

    lax.fori_loop(0, rows, issue, 0)
    pltpu.make_async_copy(x_hbm.at[pl.ds(0, rows), :], buf_ref, sem).wait()
    o_ref[...] = buf_ref[...].astype(BF16)


def _moe_gather(x, src, *, rows=256):
    n_rows = src.shape[0]
    d = x.shape[1]
    assert n_rows % rows == 0
    return pl.pallas_call(
        functools.partial(_moe_gather_kernel, rows=rows),
        grid=(n_rows // rows,),
        in_specs=[pl.BlockSpec((None, 1, rows), lambda i: (i, 0, 0), memory_space=pltpu.SMEM),
                  pl.BlockSpec(memory_space=pl.ANY)],
        out_specs=pl.BlockSpec((rows, d), lambda i: (i, 0)),
        out_shape=jax.ShapeDtypeStruct((n_rows, d), BF16),
        scratch_shapes=[pltpu.VMEM((rows, d), F32), pltpu.SemaphoreType.DMA(())],
        compiler_params=_cparams("arbitrary"),
        name="moe_gather",
    )(src.reshape(n_rows // rows, 1, rows), x)


def _moe_up_kernel(te_ref, x_ref, wg_ref, wu_ref, wrow_ref, o_ref):
    x = x_ref[...]
    h = _silu(_dot(x, wg_ref[...])) * _dot(x, wu_ref[...])
    o_ref[...] = (h * wrow_ref[...]).astype(BF16)


def _moe_up(xs, wg, wu, wrow, tile_expert, *, tn=896):
    n_rows, k = xs.shape
    n_e = wg.shape[-1]
    w_spec = pl.BlockSpec((None, k, tn), lambda j, i, te: (te[i], 0, j))
    return pl.pallas_call(
        _moe_up_kernel,
        grid_spec=pltpu.PrefetchScalarGridSpec(
            num_scalar_prefetch=1,
            grid=(n_e // tn, n_rows // MOE_TILE),
            in_specs=[pl.BlockSpec((MOE_TILE, k), lambda j, i, te: (i, 0)), w_spec, w_spec,
                      pl.BlockSpec((MOE_TILE, 1), lambda j, i, te: (i, 0))],
            out_specs=pl.BlockSpec((MOE_TILE, tn), lambda j, i, te: (i, j))),
        out_shape=jax.ShapeDtypeStruct((n_rows, n_e), BF16),
        compiler_params=_cparams("arbitrary", "arbitrary"),
        name="moe_up",
    )(tile_expert, xs, wg, wu, wrow.reshape(n_rows, 1))


def _moe_down_kernel(te_ref, h_ref, w_ref, o_ref):
    o_ref[...] = _dot(h_ref[...], w_ref[...])


def _moe_down(h, wd, tile_expert, *, tn=1024):
    n_rows, k = h.shape
    n = wd.shape[-1]
    return pl.pallas_call(
        _moe_down_kernel,
        grid_spec=pltpu.PrefetchScalarGridSpec(
            num_scalar_prefetch=1,
            grid=(n // tn, n_rows // MOE_TILE),
            in_specs=[pl.BlockSpec((MOE_TILE, k), lambda j, i, te: (i, 0)),
                      pl.BlockSpec((None, k, tn), lambda j, i, te: (te[i], 0, j))],
            out_specs=pl.BlockSpec((MOE_TILE, tn), lambda j, i, te: (i, j))),
        out_shape=jax.ShapeDtypeStruct((n_rows, n), F32),
        compiler_params=_cparams("arbitrary", "arbitrary"),
        name="moe_down",
    )(tile_expert, h, wd)


def _moe_combine_kernel(p1_ref, p2_ref, y_hbm, x_ref, w_ref, b_ref, o_ref, ob_ref, buf1, buf2, sem1, sem2, *, rows):
    def issue(r, c):
        _row_copy(y_hbm, p1_ref[0, r], buf1, r, sem1).start()
        _row_copy(y_hbm, p2_ref[0, r], buf2, r, sem2).start()
        return c

    lax.fori_loop(0, rows, issue, 0)
    pltpu.make_async_copy(y_hbm.at[pl.ds(0, rows), :], buf1, sem1).wait()
    pltpu.make_async_copy(y_hbm.at[pl.ds(0, rows), :], buf2, sem2).wait()
    x = DEEPNORM_ALPHA * x_ref[...] + (buf1[...] + buf2[...])
    mu = jnp.mean(x, axis=-1, keepdims=True)
    xc = x - mu
    var = jnp.mean(xc * xc, axis=-1, keepdims=True)
    y = xc * lax.rsqrt(var + 1e-5) * w_ref[...] + b_ref[...]
    o_ref[...] = y
    ob_ref[...] = y.astype(BF16)


def _moe_combine(ys, pos1, pos2, x, ln_w, ln_b, *, rows=128):
    t_all, d = x.shape
    rows = min(rows, t_all)
    idx = pl.BlockSpec((None, 1, rows), lambda i: (i, 0, 0), memory_space=pltpu.SMEM)
    row = pl.BlockSpec((rows, d), lambda i: (i, 0))
    vec = pl.BlockSpec((1, d), lambda i: (0, 0))
    return pl.pallas_call(
        functools.partial(_moe_combine_kernel, rows=rows),
        grid=(t_all // rows,),
        in_specs=[idx, idx, pl.BlockSpec(memory_space=pl.ANY), row, vec, vec],
        out_specs=[row, row],
        out_shape=[jax.ShapeDtypeStruct((t_all, d), F32), jax.ShapeDtypeStruct((t_all, d), BF16)],
        scratch_shapes=[pltpu.VMEM((rows, d), F32), pltpu.VMEM((rows, d), F32),
                        pltpu.SemaphoreType.DMA(()), pltpu.SemaphoreType.DMA(())],
        compiler_params=_cparams("arbitrary"),
        name="moe_combine",
    )(pos1.reshape(t_all // rows, 1, rows), pos2.reshape(t_all // rows, 1, rows), ys, x,
      ln_w.reshape(1, d), ln_b.reshape(1, d))


_CONV_HALO = SUBLANES


def _gdn_prep_kernel(x_ref, w_ref, o_ref, buf_ref, *, rows):
    @pl.when(pl.program_id(1) == 0)
    def _():
        buf_ref[0:_CONV_HALO, :] = jnp.zeros((_CONV_HALO, buf_ref.shape[1]), F32)

    buf_ref[_CONV_HALO:_CONV_HALO + rows, :] = x_ref[...].astype(F32)
    n_strips = x_ref.shape[1] // HEAD_DIM
    for s in range(n_strips):
        cols = slice(s * HEAD_DIM, (s + 1) * HEAD_DIM)
        acc = w_ref[GDN_CONV - 1:GDN_CONV, cols] * buf_ref[_CONV_HALO:_CONV_HALO + rows, cols]
        for i in range(GDN_CONV - 1):
            start = _CONV_HALO - (GDN_CONV - 1) + i
            acc = acc + w_ref[i:i + 1, cols] * buf_ref[start:start + rows, cols]
        y = _silu(acc)
        if s < 2 * N_HEADS:
            y = y * lax.rsqrt(jnp.sum(y * y, axis=-1, keepdims=True) + 1e-6)
            if s < N_HEADS:
                y = y * HEAD_DIM ** -0.5
        o_ref[:, cols] = y
    buf_ref[0:_CONV_HALO, :] = buf_ref[rows:rows + _CONV_HALO, :]


def _gdn_prep(proj, conv_w, *, batch, seq, rows=256):
    rows = min(rows, seq)
    width = 3 * GROUP_WIDTH
    nblk = seq // rows
    return pl.pallas_call(
        functools.partial(_gdn_prep_kernel, rows=rows),
        grid=(batch, nblk),
        in_specs=[pl.BlockSpec((rows, width), lambda b, s: (b * nblk + s, OFF_GDN_QKV // width)),
                  pl.BlockSpec((GDN_CONV, width), lambda b, s: (0, 0))],
        out_specs=pl.BlockSpec((rows, width), lambda b, s: (b * nblk + s, 0)),
        out_shape=jax.ShapeDtypeStruct((batch * seq, width), F32),
        scratch_shapes=[pltpu.VMEM((rows + _CONV_HALO, width), F32)],
        compiler_params=_cparams("parallel", "arbitrary"),
        name="gdn_prep",
    )(proj, conv_w)


def _split2(x):
    hi = x.astype(BF16)
    return hi, (x - hi.astype(F32)).astype(BF16)


def _split3(x):
    hi = x.astype(BF16)
    r = x - hi.astype(F32)
    mid = r.astype(BF16)
    return hi, mid, (r - mid.astype(F32)).astype(BF16)


def _dot_x3(a, b):
    a_hi, a_lo = _split2(a)
    b_hi, b_lo = _split2(b)
    return _dot(a_hi, b_hi) + (_dot(a_hi, b_lo) + _dot(a_lo, b_hi))


def _gdn_kernel(q_ref, k_ref, v_ref, z_ref, misc_ref, alog_ref, dtb_ref, nw_ref, o_ref, s_ref, *, n_chunks, hg):
    c_len = GDN_CHUNK
    h0 = pl.program_id(1) * hg

    @pl.when(pl.program_id(2) == 0)
    def _():
        s_ref[...] = jnp.zeros_like(s_ref)

    r_io = lax.broadcasted_iota(jnp.int32, (c_len, c_len), 0)
    c_io = lax.broadcasted_iota(jnp.int32, (c_len, c_len), 1)
    incl = r_io >= c_io
    strict = r_io > c_io
    l_incl = incl.astype(BF16)
    eye = (r_io == c_io).astype(F32)
    sel = lax.broadcasted_iota(jnp.int32, (LANES, LANES), 0)
    lane0 = (lax.broadcasted_iota(jnp.int32, (c_len, LANES), 1) == 0).astype(BF16)
    heads = range(hg)
    chunks = range(n_chunks)
    pairs = [(hh, c) for hh in heads for c in chunks]
    rows = [pl.ds(c * c_len, c_len) for c in chunks]
    cols = [slice(hh * HEAD_DIM, (hh + 1) * HEAD_DIM) for hh in heads]
    pick_a = [(sel == MISC_A + h0 + hh).astype(BF16) for hh in heads]
    pick_b = [(sel == MISC_B + h0 + hh).astype(BF16) for hh in heads]
    neg_a = [-jnp.exp(alog_ref[hh]) for hh in heads]
    misc = [misc_ref[rows[c], :] for c in chunks]

    def softplus(t):
        return jnp.maximum(t, 0.0) + jnp.log1p(jnp.exp(-jnp.abs(t)))

    g_b = {p: neg_a[p[0]] * softplus(_dot(misc[p[1]], pick_a[p[0]]) + dtb_ref[p[0]]) for p in pairs}
    beta_b = {p: jax.nn.sigmoid(_dot(misc[p[1]], pick_b[p[0]])) for p in pairs}
    gc_b = {p: sum(_dot(l_incl, part) for part in _split3(g_b[p])) for p in pairs}
    gc_row = {p: sum(_dot_nt(lane0, part) for part in _split3(gc_b[p])) for p in pairs}
    decay = {p: jnp.where(incl, jnp.exp(gc_b[p][:, :c_len] - gc_row[p]), 0.0) for p in pairs}

    q = {p: q_ref[rows[p[1]], cols[p[0]]] for p in pairs}
    k = {p: k_ref[rows[p[1]], cols[p[0]]] for p in pairs}
    k_bf = {p: k[p].astype(BF16) for p in pairs}
    kb = {p: k[p] * beta_b[p] for p in pairs}
    a_mat = {p: jnp.where(strict, _dot_nt(kb[p].astype(BF16), k_bf[p]) * decay[p], 0.0) for p in pairs}
    t_inv = {p: eye - a_mat[p] for p in pairs}
    a_pow = {p: _dot_x3(a_mat[p], a_mat[p]) for p in pairs}
    n_doublings = int(math.log2(c_len)) - 1
    for it in range(n_doublings):
        t_inv = {p: t_inv[p] + _dot_x3(t_inv[p], a_pow[p]) for p in pairs}
        if it < n_doublings - 1:
            a_pow = {p: _dot_x3(a_pow[p], a_pow[p]) for p in pairs}
    eg = {p: jnp.exp(gc_b[p]) for p in pairs}
    sol = {p: _dot_x3(t_inv[p], jnp.concatenate([kb[p] * eg[p], v_ref[rows[p[1]], cols[p[0]]] * beta_b[p]], axis=1))
           for p in pairs}
    w_bf = {p: sol[p][:, :HEAD_DIM].astype(BF16) for p in pairs}
    qk_bf = {p: (_dot_nt(q[p].astype(BF16), k_bf[p]) * decay[p]).astype(BF16) for p in pairs}
    qd_bf = {p: (q[p] * eg[p]).astype(BF16) for p in pairs}
    g_last = {p: gc_b[p][c_len - 1:c_len, :] for p in pairs}
    kd_bf = {p: (k[p] * jnp.exp(g_last[p] - gc_b[p])).astype(BF16) for p in pairs}
    eg_last = {p: jnp.exp(g_last[p]) for p in pairs}

    state = [s_ref[hh] for hh in heads]
    for c in chunks:
        state_bf = [state[hh].astype(BF16) for hh in heads]
        v_new = [sol[hh, c][:, HEAD_DIM:] - _dot(w_bf[hh, c], state_bf[hh]) for hh in heads]
        v_new_bf = [t.astype(BF16) for t in v_new]
        out = [_dot(qd_bf[hh, c], state_bf[hh]) + _dot(qk_bf[hh, c], v_new_bf[hh]) for hh in heads]
        state = [state[hh] * eg_last[hh, c] + _dot_tn(kd_bf[hh, c], v_new_bf[hh]) for hh in heads]
        for hh in heads:
            y = out[hh] * lax.rsqrt(jnp.mean(out[hh] * out[hh], axis=-1, keepdims=True) + 1e-6) * nw_ref[...]
            o_ref[rows[c], cols[hh]] = (y * _silu(z_ref[rows[c], cols[hh]].astype(F32))).astype(BF16)
    for hh in heads:
        s_ref[hh] = state[hh]


def _gdn(qkv, proj, a_log, dt_bias, norm_w, *, batch, seq, rows=256, hg=4):
    rows = min(rows, seq)
    nblk = seq // rows
    width = hg * HEAD_DIM
    groups = N_HEADS // hg

    def head_spec(col0):
        return pl.BlockSpec((rows, width), lambda b, g, s: (b * nblk + s, col0 // hg + g))

    per_head = pl.BlockSpec((hg, 1, LANES), lambda b, g, s: (g, 0, 0))
    bcast = lambda t: jnp.broadcast_to(t.astype(F32)[:, None, None], (N_HEADS, 1, LANES))
    return pl.pallas_call(
        functools.partial(_gdn_kernel, n_chunks=rows // GDN_CHUNK, hg=hg),
        grid=(batch, groups, nblk),
        in_specs=[head_spec(0), head_spec(N_HEADS), head_spec(2 * N_HEADS),
                  head_spec(OFF_GDN_Z // HEAD_DIM),
                  pl.BlockSpec((rows, LANES), lambda b, g, s: (b * nblk + s, OFF_MISC // LANES)),
                  per_head, per_head,
                  pl.BlockSpec((1, HEAD_DIM), lambda b, g, s: (0, 0))],
        out_specs=head_spec(0),
        out_shape=jax.ShapeDtypeStruct((batch * seq, GROUP_WIDTH), BF16),
        scratch_shapes=[pltpu.VMEM((hg, HEAD_DIM, HEAD_DIM), F32)],
        compiler_params=_cparams("parallel", "parallel", "arbitrary"),
        name="gdn",
    )(qkv, qkv, qkv, proj, proj, bcast(a_log), bcast(dt_bias), norm_w.reshape(1, HEAD_DIM))


_MLA_QK_PAD = 2 * HEAD_DIM
_MLA_IN = MLA_Q_RANK + LANES + MLA_KV_RANK


def _mla_prep_kernel(x_ref, qnw_ref, kvnw_ref, wuq_ref, wukv_ref, cos_ref, sin_ref, q_ref, k_ref, v_ref):
    x = x_ref[...]
    cq = x[:, :MLA_Q_RANK].astype(F32)
    misc = x[:, MLA_Q_RANK:MLA_Q_RANK + LANES].astype(F32)
    ckv = x[:, MLA_Q_RANK + LANES:].astype(F32)

    def rms(t, w):
        return t * lax.rsqrt(jnp.mean(t * t, axis=-1, keepdims=True) + 1e-6) * w

    mq = _dot(rms(cq, qnw_ref[...]).astype(BF16), wuq_ref[...])
    mkv = _dot(rms(ckv, kvnw_ref[...]).astype(BF16), wukv_ref[...])
    cos = cos_ref[...]
    sin = sin_ref[...]
    lane = lax.broadcasted_iota(jnp.int32, cos.shape, 1)
    first_half = (lane % MLA_ROPE_DIM) < MLA_ROPE_DIM // 2
    low = lane < MLA_ROPE_DIM

    def rope(t):
        swapped = jnp.where(first_half, pltpu.roll(t, LANES - MLA_ROPE_DIM // 2, 1),
                            pltpu.roll(t, MLA_ROPE_DIM // 2, 1))
        return t * cos + swapped * sin

    scale = (MLA_NOPE_DIM + MLA_ROPE_DIM) ** -0.5 * math.log2(math.e)
    k_rope = jnp.where(low, rope(misc), 0.0).astype(BF16)
    nope_w = N_HEADS * MLA_NOPE_DIM
    for h in range(N_HEADS):
        pair = rope(mq[:, nope_w + (h // 2) * LANES: nope_w + (h // 2 + 1) * LANES])
        q_rope = pair if h % 2 == 0 else pltpu.roll(pair, MLA_ROPE_DIM, 1)
        q_rope = jnp.where(low, q_rope, 0.0)
        base = h * _MLA_QK_PAD
        q_ref[:, base:base + HEAD_DIM] = (mq[:, h * HEAD_DIM:(h + 1) * HEAD_DIM] * scale).astype(BF16)
        q_ref[:, base + HEAD_DIM:base + _MLA_QK_PAD] = (q_rope * scale).astype(BF16)
        k_ref[:, base:base + HEAD_DIM] = mkv[:, h * HEAD_DIM:(h + 1) * HEAD_DIM].astype(BF16)
        k_ref[:, base + HEAD_DIM:base + _MLA_QK_PAD] = k_rope
    v_ref[...] = mkv[:, nope_w:].T.astype(BF16)


def _mla_prep(proj, q_norm_w, kv_norm_w, w_uq, w_ukv, cos, sin, *, seq, tm=512):
    t_all = proj.shape[0]
    tm = min(tm, seq)
    nblk = seq // tm
    full = lambda a: pl.BlockSpec(a.shape, lambda i: (0,) * a.ndim)
    table = pl.BlockSpec((tm, LANES), lambda i: (i % nblk, 0))
    qnw = q_norm_w.reshape(1, -1)
    kvnw = kv_norm_w.reshape(1, -1)
    wide = N_HEADS * _MLA_QK_PAD
    return pl.pallas_call(
        _mla_prep_kernel,
        grid=(t_all // tm,),
        in_specs=[pl.BlockSpec((tm, _MLA_IN), lambda i: (i, OFF_MLA // _MLA_IN)),
                  full(qnw), full(kvnw), full(w_uq), full(w_ukv), table, table],
        out_specs=[pl.BlockSpec((tm, wide), lambda i: (i, 0)),
                   pl.BlockSpec((tm, wide), lambda i: (i, 0)),
                   pl.BlockSpec((GROUP_WIDTH, tm), lambda i: (0, i))],
        out_shape=[jax.ShapeDtypeStruct((t_all, wide), BF16),
                   jax.ShapeDtypeStruct((t_all, wide), BF16),
                   jax.ShapeDtypeStruct((GROUP_WIDTH, t_all), BF16)],
        compiler_params=_cparams("parallel"),
        name="mla_prep",
    )(proj, qnw, kvnw, w_uq, w_ukv, cos, sin)


def _mla_attn_kernel(q_ref, k_ref, vt_ref, o_ref, m_ref, l_ref, acc_ref, *, t):
    i = pl.program_id(2)
    m_ref[...] = jnp.full_like(m_ref, -jnp.inf)
    l_ref[...] = jnp.zeros_like(l_ref)
    acc_ref[...] = jnp.zeros_like(acc_ref)

    def tile(kb, diagonal):
        start = pl.multiple_of(kb * t, t)
        s = _dot_nt(k_ref[pl.ds(start, t), :], q_ref[...])
        if diagonal:
            key = lax.broadcasted_iota(jnp.int32, s.shape, 0)
            qry = lax.broadcasted_iota(jnp.int32, s.shape, 1)
            s = jnp.where(key <= qry, s, -jnp.inf)
        m_prev = m_ref[...]
        m_new = jnp.maximum(m_prev, jnp.max(s, axis=0, keepdims=True))
        p = jnp.exp2(s - m_new)
        corr = jnp.exp2(m_prev - m_new)
        l_ref[...] = corr * l_ref[...] + jnp.sum(p, axis=0, keepdims=True)
        acc_ref[...] = corr * acc_ref[...] + _dot(vt_ref[:, pl.ds(start, t)], p.astype(BF16))
        m_ref[...] = m_new

    def body(kb, carry):
        tile(kb, False)
        return carry

    lax.fori_loop(0, i, body, 0)
    tile(i, True)
    o_ref[...] = (acc_ref[...] / l_ref[...]).T.astype(o_ref.dtype)


def _mla_attn(q, k, v, *, batch, seq, t=512):
    t = min(t, seq)
    nblk = seq // t
    return pl.pallas_call(
        functools.partial(_mla_attn_kernel, t=t),
        grid=(batch, N_HEADS, nblk),
        in_specs=[pl.BlockSpec((t, _MLA_QK_PAD), lambda b, h, i: (b * nblk + i, h)),
                  pl.BlockSpec((seq, _MLA_QK_PAD), lambda b, h, i: (b, h)),
                  pl.BlockSpec((HEAD_DIM, seq), lambda b, h, i: (h, b))],
        out_specs=pl.BlockSpec((t, HEAD_DIM), lambda b, h, i: (b * nblk + i, h)),
        out_shape=jax.ShapeDtypeStruct((batch * seq, GROUP_WIDTH), BF16),
        scratch_shapes=[pltpu.VMEM((1, t), F32), pltpu.VMEM((1, t), F32), pltpu.VMEM((HEAD_DIM, t), F32)],
        compiler_params=_cparams("parallel", "parallel", "arbitrary"),
        name="mla_attn",
    )(q, k, v)


def _sgu_kernel(x_ref, nw_ref, nb_ref, ws_ref, bs_ref, o_ref, *, n_chunks):
    x = x_ref[...].astype(F32)
    ge = 0.5 * x * (1.0 + lax.erf(x * np.float32(math.sqrt(0.5))))
    u = ge[:, :GROUP_WIDTH]
    sv = ge[:, GROUP_WIDTH:]
    mu = jnp.mean(sv, axis=-1, keepdims=True)
    svc = sv - mu
    var = jnp.mean(svc * svc, axis=-1, keepdims=True)
    svn = (svc * lax.rsqrt(var + 1e-5) * nw_ref[...] + nb_ref[...]).astype(BF16)
    r_io = lax.broadcasted_iota(jnp.int32, (SGU_CHUNK, SGU_CHUNK), 0)
    c_io = lax.broadcasted_iota(jnp.int32, (SGU_CHUNK, SGU_CHUNK), 1)
    causal = r_io >= c_io
    gd = GROUP_WIDTH // SGU_GROUPS
    for g in range(SGU_GROUPS):
        w_mix = jnp.where(causal, ws_ref[g], 0.0).astype(BF16)
        bias = bs_ref[:, g:g + 1]
        cols = slice(g * gd, (g + 1) * gd)
        for c in range(n_chunks):
            rows = slice(c * SGU_CHUNK, (c + 1) * SGU_CHUNK)
            mixed = _dot(w_mix, svn[rows, cols]) + bias
            o_ref[rows, cols] = (u[rows, cols] * mixed).astype(BF16)


def _sgu(proj, norm_w, norm_b, w_s, b_s, *, rows=256):
    t_all = proj.shape[0]
    rows = min(rows, t_all)
    width = 2 * GROUP_WIDTH
    bs_t = jnp.pad(b_s.T, ((0, 0), (0, LANES - SGU_GROUPS)))
    full = lambda a: pl.BlockSpec(a.shape, lambda i: (0,) * a.ndim)
    nw = norm_w.reshape(1, -1)
    nb = norm_b.reshape(1, -1)
    return pl.pallas_call(
        functools.partial(_sgu_kernel, n_chunks=rows // SGU_CHUNK),
        grid=(t_all // rows,),
        in_specs=[pl.BlockSpec((rows, width), lambda i: (i, OFF_SGU // width)),
                  full(nw), full(nb), full(w_s), full(bs_t)],
        out_specs=pl.BlockSpec((rows, GROUP_WIDTH), lambda i: (i, 0)),
        out_shape=jax.ShapeDtypeStruct((t_all, GROUP_WIDTH), BF16),
        compiler_params=_cparams("parallel"),
        name="sgu",
    )(proj, nw, nb, w_s, bs_t)


_SB_SUB = 256


def _sb_kernel(q_ref, k_ref, v_ref, o_ref, carry_ref, acc_ref, *, t):
    i = pl.program_id(2)
    sub = min(_SB_SUB, t)
    carry_ref[...] = jnp.zeros_like(carry_ref)
    acc_ref[...] = jnp.zeros_like(acc_ref)
    sr = lax.broadcasted_iota(jnp.int32, (2 * sub, sub), 0)
    sc = lax.broadcasted_iota(jnp.int32, (2 * sub, sub), 1)
    suffix2 = (jnp.where(sr >= sub, sr - sub, sr) > sc).astype(BF16)

    def tile(kb, diagonal):
        start = pl.multiple_of(kb * t, t)
        z = _dot_nt(q_ref[...], k_ref[pl.ds(start, t), :]) * np.float32(HEAD_DIM ** -0.5)
        log_beta = jnp.minimum(z, 0.0) - jnp.log(1.0 + jnp.exp(-jnp.abs(z)))
        log_rest = log_beta - z
        if diagonal:
            strict = lax.broadcasted_iota(jnp.int32, z.shape, 1) < lax.broadcasted_iota(jnp.int32, z.shape, 0)
            log_rest = jnp.where(strict, log_rest, 0.0)
        carry = carry_ref[...]
        acc = acc_ref[...]
        for sb in reversed(range(t // sub)):
            cs = slice(sb * sub, (sb + 1) * sub)
            lr = log_rest[:, cs]
            hi, lo = _split2(lr)
            tail = _dot(jnp.concatenate([hi, lo], axis=1), suffix2)
            a = jnp.exp(log_beta[:, cs] + (tail + carry))
            if diagonal:
                a = jnp.where(strict[:, cs], a, 0.0)
            acc = acc + _dot(a.astype(BF16), v_ref[pl.ds(start + sb * sub, sub), :])
            carry = carry + (tail[:, 0:1] + lr[:, 0:1])
        carry_ref[...] = carry
        acc_ref[...] = acc

    tile(i, True)

    def body(jj, c):
        tile(i - 1 - jj, False)
        return c

    lax.fori_loop(0, i, body, 0)
    o_ref[...] = acc_ref[...].astype(o_ref.dtype)


def _sb_attn(proj, *, batch, seq, t=512):
    t = min(t, seq)
    nblk = seq // t
    c0 = OFF_SB // HEAD_DIM
    return pl.pallas_call(
        functools.partial(_sb_kernel, t=t),
        grid=(batch, N_HEADS, nblk),
        in_specs=[pl.BlockSpec((t, HEAD_DIM), lambda b, h, i: (b * nblk + i, c0 + h)),
                  pl.BlockSpec((seq, HEAD_DIM), lambda b, h, i: (b, c0 + N_HEADS + h)),
                  pl.BlockSpec((seq, HEAD_DIM), lambda b, h, i: (b, c0 + 2 * N_HEADS + h))],
        out_specs=pl.BlockSpec((t, HEAD_DIM), lambda b, h, i: (b * nblk + i, h)),
        out_shape=jax.ShapeDtypeStruct((batch * seq, GROUP_WIDTH), BF16),
        scratch_shapes=[pltpu.VMEM((t, 1), F32), pltpu.VMEM((t, HEAD_DIM), F32)],
        compiler_params=_cparams("parallel", "parallel", "arbitrary"),
        name="sb_attn",
    )(proj, proj, proj)


def _reorder_w_in(w):
    wb = w.astype(BF16)
    seg = lambda a, n: wb[:, a:a + n]
    gdn_qkvz = seg(0, 4096)
    gdn_a, gdn_b = seg(4096, 8), seg(4104, 8)
    mla_cq, mla_ckv, mla_kr = seg(4112, MLA_Q_RANK), seg(5008, MLA_KV_RANK), seg(5520, MLA_ROPE_DIM)
    sgu, sb = seg(5584, 2048), seg(7632, 3072)
    pad = jnp.zeros((w.shape[0], LANES - MLA_ROPE_DIM - 2 * N_HEADS), BF16)
    return jnp.concatenate([gdn_qkvz, sgu, sb, mla_cq, mla_kr, gdn_a, gdn_b, pad, mla_ckv], axis=1)


def _split_heads_cols(w, first):
    k = w.shape[0]
    w3 = w.reshape(k, N_HEADS, -1)
    return jnp.concatenate([w3[:, :, :first].reshape(k, -1), w3[:, :, first:].reshape(k, -1)], axis=1)


def _rope_tables(seq):
    half = MLA_ROPE_DIM // 2
    pos = jnp.arange(seq, dtype=F32)
    inv_freq = ROPE_THETA ** (-jnp.arange(half, dtype=F32) / half)
    ang = pos[:, None] * inv_freq[None, :]
    cos, sin = jnp.cos(ang), jnp.sin(ang)
    reps = LANES // half
    sign = jnp.tile(jnp.concatenate([-jnp.ones((half,), F32), jnp.ones((half,), F32)]), LANES // MLA_ROPE_DIM)
    return jnp.tile(cos, (1, reps)), jnp.tile(sin, (1, reps)) * sign[None, :]


def kernel(x, w_in, gdn_conv_w, gdn_a_log, gdn_dt_bias, gdn_norm_w, mla_q_norm_w, mla_w_uq, mla_kv_norm_w, mla_w_ukv, sgu_norm_w, sgu_norm_b, sgu_w_s, sgu_b_s, w_out, ln_mix_w, ln_mix_b, ffn_w_gate, ffn_w_up, ffn_w_down, moe_router_w, moe_router_b, moe_w_gate, moe_w_up, moe_w_down, ln_ffn_w, ln_ffn_b):
    batch, seq, d = x.shape
    t_all = batch * seq
    cos, sin = _rope_tables(seq)
    xf = x.reshape(t_all, d)
    xb = xf.astype(BF16)
    ff_pad = D_FF_PAD - D_FF
    for layer in range(DEPTH):
        proj = _matmul([xb], _reorder_w_in(w_in[layer]), tm=1024, tn=768, tk=d, out_dtype=BF16, name="in_proj")
        qkv = _gdn_prep(proj, gdn_conv_w[layer], batch=batch, seq=seq)
        o_a = _gdn(qkv, proj, gdn_a_log[layer], gdn_dt_bias[layer], gdn_norm_w[layer], batch=batch, seq=seq)
        mq, mk, mv = _mla_prep(proj, mla_q_norm_w[layer], mla_kv_norm_w[layer],
                               _split_heads_cols(mla_w_uq[layer], MLA_NOPE_DIM).astype(BF16),
                               _split_heads_cols(mla_w_ukv[layer], MLA_NOPE_DIM).astype(BF16),
                               cos, sin, seq=seq)
        o_b = _mla_attn(mq, mk, mv, batch=batch, seq=seq)
        o_c = _sgu(proj, sgu_norm_w[layer], sgu_norm_b[layer], sgu_w_s[layer], sgu_b_s[layer])
        o_d = _sb_attn(proj, batch=batch, seq=seq)
        pre = _matmul([o_a, o_b, o_c, o_d], w_out[layer].astype(BF16), tm=1024, tn=512, tk=GROUP_WIDTH,
                      out_dtype=F32, res=xf, alpha=DEEPNORM_ALPHA, name="out_proj")
        xf, xb = _layer_norm(pre, ln_mix_w[layer], ln_mix_b[layer], name="ln_mix")
        i = layer // 2
        if layer % 2 == 0:
            wg = jnp.pad(ffn_w_gate[i].astype(BF16), ((0, 0), (0, ff_pad)))
            wu = jnp.pad(ffn_w_up[i].astype(BF16), ((0, 0), (0, ff_pad)))
            wd = jnp.pad(ffn_w_down[i].astype(BF16), ((0, ff_pad), (0, 0)))
            hid = _gate_up(xb, wg, wu, tm=1024, tn=512, name="ffn_gate_up")
            pre = _matmul([hid], wd, tm=1024, tn=1024, tk=2816, out_dtype=F32, res=xf, alpha=DEEPNORM_ALPHA,
                          name="ffn_down")
            xf, xb = _layer_norm(pre, ln_ffn_w[layer], ln_ffn_b[layer], name="ln_ffn")
        else:
            meta, counts = _router(xf, moe_router_w[i], moe_router_b[i])
            pos1, pos2, src, wrow, tile_expert = _moe_plan(meta, counts, t_all)
            xs = _moe_gather(xf, src)
            hid = _moe_up(xs, moe_w_gate[i].astype(BF16), moe_w_up[i].astype(BF16), wrow, tile_expert)
            ys = _moe_down(hid, moe_w_down[i].astype(BF16), tile_expert)
            xf, xb = _moe_combine(ys, pos1, pos2, xf, ln_ffn_w[layer], ln_ffn_b[layer])
    return xf.reshape(batch, seq, d)
```

```python
import functools
import math

import jax
import jax.numpy as jnp
import numpy as np
from jax import lax
from jax.experimental import pallas as pl
from jax.experimental.pallas import tpu as pltpu

F32 = jnp.float32
BF16 = jnp.bfloat16
HIGHEST = lax.Precision.HIGHEST

D_MODEL = 4096
DEPTH = 4
HEAD_DIM = 128
GROUP_WIDTH = 1024
N_HEADS = 8
GDN_CONV = 4
GDN_CHUNK = 64
MLA_Q_RANK = 896
MLA_KV_RANK = 512
MLA_NOPE_DIM = 128
MLA_ROPE_DIM = 64
ROPE_THETA = 10000.0
SGU_CHUNK = 128
SGU_GROUPS = 8
D_FF = 11008
N_EXPERTS = 8
D_EXPERT = 1792
DEEPNORM_ALPHA = (2 * DEPTH) ** 0.25

LANES = 128
SUBLANES = 8
VMEM_LIMIT_BYTES = 56 * 1024 * 1024

OFF_GDN_QKV = 0
OFF_GDN_Z = 3072
OFF_SGU = 4096
OFF_SB = 6144
OFF_MLA = 9216
OFF_MISC = OFF_MLA + MLA_Q_RANK
D_IN_PAD = 10752
MISC_A = 64
MISC_B = 72
D_FF_PAD = 11264


def _cparams(*sem):
    return pltpu.CompilerParams(dimension_semantics=sem, vmem_limit_bytes=VMEM_LIMIT_BYTES)


def _dot(a, b):
    return jnp.dot(a, b, preferred_element_type=F32)


def _dot_nt(a, b, precision=None):
    return lax.dot_general(a, b, (((1,), (1,)), ((), ())), preferred_element_type=F32, precision=precision)


def _dot_tn(a, b):
    return lax.dot_general(a, b, (((0,), (0,)), ((), ())), preferred_element_type=F32)


def _silu(x):
    return x * jax.nn.sigmoid(x)


def _mm_kernel(*refs, n_a, nk, alpha, has_res):
    a_refs = refs[:n_a]
    w_refs = refs[n_a:2 * n_a]
    pos = 2 * n_a
    res_ref = refs[pos] if has_res else None
    pos += int(has_res)
    o_ref = refs[pos]
    acc_ref = refs[pos + 1] if nk > 1 else None

    def partial_product():
        acc = _dot(a_refs[0][...], w_refs[0][...])
        for a_ref, w_ref in zip(a_refs[1:], w_refs[1:]):
            acc = acc + _dot(a_ref[...], w_ref[...])
        return acc

    def epilogue(acc):
        if has_res:
            acc = acc + alpha * res_ref[...]
        o_ref[...] = acc.astype(o_ref.dtype)

    if nk == 1:
        epilogue(partial_product())
    else:
        k = pl.program_id(2)

        @pl.when(k == 0)
        def _():
            acc_ref[...] = jnp.zeros_like(acc_ref)

        acc_ref[...] += partial_product()

        @pl.when(k == nk - 1)
        def _():
            epilogue(acc_ref[...])


def _matmul(a_list, w, *, tm, tn, tk, out_dtype, res=None, alpha=None, name="matmul"):
    n_a = len(a_list)
    m, k_seg = a_list[0].shape
    n = w.shape[-1]
    tm, tn, tk = min(tm, m), min(tn, n), min(tk, k_seg)
    nk = k_seg // tk
    assert m % tm == 0 and n % tn == 0 and k_seg % tk == 0
    assert n_a == 1 or nk == 1
    in_specs = [pl.BlockSpec((tm, tk), lambda i, j, k: (i, k)) for _ in a_list]
    in_specs += [pl.BlockSpec((tk, tn), functools.partial(lambda i, j, k, s: (s * nk + k, j), s=s))
                 for s in range(n_a)]
    operands = list(a_list) + [w] * n_a
    if res is not None:
        in_specs.append(pl.BlockSpec((tm, tn), lambda i, j, k: (i, j)))
        operands.append(res)
    return pl.pallas_call(
        functools.partial(_mm_kernel, n_a=n_a, nk=nk, alpha=alpha, has_res=res is not None),
        grid=(m // tm, n // tn, nk),
        in_specs=in_specs,
        out_specs=pl.BlockSpec((tm, tn), lambda i, j, k: (i, j)),
        out_shape=jax.ShapeDtypeStruct((m, n), out_dtype),
        scratch_shapes=[pltpu.VMEM((tm, tn), F32)] if nk > 1 else [],
        compiler_params=_cparams("parallel", "parallel", "arbitrary"),
        name=name,
    )(*operands)


def _gate_up_kernel(x_ref, wg_ref, wu_ref, o_ref):
    x = x_ref[...]
    o_ref[...] = (_silu(_dot(x, wg_ref[...])) * _dot(x, wu_ref[...])).astype(o_ref.dtype)


def _gate_up(x, wg, wu, *, tm, tn, name="gate_up"):
    m, k = x.shape
    n = wg.shape[-1]
    tm, tn = min(tm, m), min(tn, n)
    assert m % tm == 0 and n % tn == 0
    w_spec = pl.BlockSpec((k, tn), lambda i, j: (0, j))
    return pl.pallas_call(
        _gate_up_kernel,
        grid=(m // tm, n // tn),
        in_specs=[pl.BlockSpec((tm, k), lambda i, j: (i, 0)), w_spec, w_spec],
        out_specs=pl.BlockSpec((tm, tn), lambda i, j: (i, j)),
        out_shape=jax.ShapeDtypeStruct((m, n), BF16),
        compiler_params=_cparams("parallel", "arbitrary"),
        name=name,
    )(x, wg, wu)


def _ln_kernel(x_ref, w_ref, b_ref, o_ref, ob_ref):
    x = x_ref[...]
    mu = jnp.mean(x, axis=-1, keepdims=True)
    xc = x - mu
    var = jnp.mean(xc * xc, axis=-1, keepdims=True)
    y = xc * lax.rsqrt(var + 1e-5) * w_ref[...] + b_ref[...]
    o_ref[...] = y
    ob_ref[...] = y.astype(BF16)


def _layer_norm(x, w, b, *, tm=256, name="layer_norm"):
    m, d = x.shape
    tm = min(tm, m)
    row = pl.BlockSpec((tm, d), lambda i: (i, 0))
    vec = pl.BlockSpec((1, d), lambda i: (0, 0))
    return pl.pallas_call(
        _ln_kernel,
        grid=(m // tm,),
        in_specs=[row, vec, vec],
        out_specs=[row, row],
        out_shape=[jax.ShapeDtypeStruct((m, d), F32), jax.ShapeDtypeStruct((m, d), BF16)],
        compiler_params=_cparams("parallel"),
        name=name,
    )(x, w.reshape(1, d), b.reshape(1, d))


META_E1, META_E2, META_W1, META_W2, META_R1, META_R2 = range(6)


def _router_kernel(x_ref, w_ref, b_ref, meta_ref, counts_ref, cnt_ref):
    @pl.when(pl.program_id(0) == 0)
    def _():
        cnt_ref[...] = jnp.zeros_like(cnt_ref)

    logits = jnp.dot(x_ref[...], w_ref[...], preferred_element_type=F32, precision=HIGHEST) + b_ref[...]
    tm = logits.shape[0]
    lane = lax.broadcasted_iota(jnp.int32, logits.shape, 1)
    neg = jnp.float32(-jnp.inf)
    logits = jnp.where(lane < N_EXPERTS, logits, neg)
    m1 = jnp.max(logits, axis=1, keepdims=True)
    i1 = jnp.min(jnp.where(logits == m1, lane, LANES), axis=1, keepdims=True)
    rest = jnp.where(lane == i1, neg, logits)
    m2 = jnp.max(rest, axis=1, keepdims=True)
    i2 = jnp.min(jnp.where(rest == m2, lane, LANES), axis=1, keepdims=True)
    e2 = jnp.exp(m2 - m1)
    denom = 1.0 + e2
    chosen = (lane == i1) | (lane == i2)
    earlier = (lax.broadcasted_iota(jnp.int32, (tm, tm), 0) > lax.broadcasted_iota(jnp.int32, (tm, tm), 1))
    before = _dot(earlier.astype(BF16), chosen.astype(BF16)) + cnt_ref[...]
    rank1 = jnp.sum(jnp.where(lane == i1, before, 0.0), axis=1, keepdims=True)
    rank2 = jnp.sum(jnp.where(lane == i2, before, 0.0), axis=1, keepdims=True)
    cnt_ref[...] += jnp.sum(chosen.astype(F32), axis=0, keepdims=True)
    fields = {META_E1: i1.astype(F32), META_E2: i2.astype(F32), META_W1: 1.0 / denom, META_W2: e2 / denom,
              META_R1: rank1, META_R2: rank2}
    meta = jnp.zeros(logits.shape, F32)
    for pos, val in fields.items():
        meta = jnp.where(lane == pos, val, meta)
    meta_ref[...] = meta
    counts_ref[...] = cnt_ref[...]


def _router(x, w, b, *, tm=512, name="router"):
    m, d = x.shape
    tm = min(tm, m)
    w_pad = jnp.pad(w, ((0, 0), (0, LANES - N_EXPERTS)))
    b_pad = jnp.pad(b, (0, LANES - N_EXPERTS)).reshape(1, LANES)
    return pl.pallas_call(
        _router_kernel,
        grid=(m // tm,),
        in_specs=[pl.BlockSpec((tm, d), lambda i: (i, 0)),
                  pl.BlockSpec((d, LANES), lambda i: (0, 0)),
                  pl.BlockSpec((1, LANES), lambda i: (0, 0))],
        out_specs=[pl.BlockSpec((tm, LANES), lambda i: (i, 0)),
                   pl.BlockSpec((1, LANES), lambda i: (0, 0))],
        out_shape=[jax.ShapeDtypeStruct((m, LANES), F32), jax.ShapeDtypeStruct((1, LANES), F32)],
        scratch_shapes=[pltpu.VMEM((1, LANES), F32)],
        compiler_params=_cparams("arbitrary"),
        name=name,
    )(x, w_pad, b_pad)


MOE_TILE = 512


def _moe_plan(meta, counts, t_all):
    as_int = lambda lane: meta[:, lane].astype(jnp.int32)
    e1, e2, r1, r2 = as_int(META_E1), as_int(META_E2), as_int(META_R1), as_int(META_R2)
    cnt = counts[0, :N_EXPERTS].astype(jnp.int32)
    padded = (cnt + MOE_TILE - 1) // MOE_TILE * MOE_TILE
    ends = jnp.cumsum(padded)
    starts = ends - padded
    pos1 = starts[e1] + r1
    pos2 = starts[e2] + r2
    n_rows = 2 * t_all + N_EXPERTS * MOE_TILE
    tok = jnp.arange(t_all, dtype=jnp.int32)
    src = jnp.zeros((n_rows,), jnp.int32).at[pos1].set(tok).at[pos2].set(tok)
    wrow = jnp.zeros((n_rows,), F32).at[pos1].set(meta[:, META_W1]).at[pos2].set(meta[:, META_W2])
    tile_start = jnp.arange(n_rows // MOE_TILE, dtype=jnp.int32) * MOE_TILE
    tile_expert = jnp.minimum(jnp.searchsorted(ends, tile_start, side="right"), N_EXPERTS - 1).astype(jnp.int32)
    tile_expert = jnp.concatenate([tile_expert, (ends[-1:] // MOE_TILE).astype(jnp.int32)])
    return pos1, pos2, src, wrow, tile_expert


def _row_copy(src_hbm, row, dst_vmem, slot, sem):
    return pltpu.make_async_copy(src_hbm.at[pl.ds(row, 1), :], dst_vmem.at[pl.ds(slot, 1), :], sem)


def _moe_gather_kernel(src_ref, x_hbm, o_ref, buf_ref, sem, *, rows):
    def issue(r, c):
        _row_copy(x_hbm, src_ref[0, r], buf_ref, r, sem).start()
        return c

    lax.fori_loop(0, rows, issue, 0, unroll=8)
    pltpu.make_async_copy(x_hbm.at[pl.ds(0, rows), :], buf_ref, sem).wait()
    o_ref[...] = buf_ref[...].astype(BF16)


def _moe_gather(x, src, *, rows=256):
    n_rows = src.shape[0]
    d = x.shape[1]
    assert n_rows % rows == 0
    return pl.pallas_call(
        functools.partial(_moe_gather_kernel, rows=rows),
        grid=(n_rows // rows,),
        in_specs=[pl.BlockSpec((None, 1, rows), lambda i: (i, 0, 0), memory_space=pltpu.SMEM),
                  pl.BlockSpec(memory_space=pl.ANY)],
        out_specs=pl.BlockSpec((rows, d), lambda i: (i, 0)),
        out_shape=jax.ShapeDtypeStruct((n_rows, d), BF16),
        scratch_shapes=[pltpu.VMEM((rows, d), F32), pltpu.SemaphoreType.DMA(())],
        compiler_params=_cparams("arbitrary"),
        name="moe_gather",
    )(src.reshape(n_rows // rows, 1, rows), x)


def _moe_tile_is_used(te_ref):
    return pl.program_id(1) < te_ref[pl.num_programs(1)]


def _moe_up_kernel(te_ref, x_ref, wg_ref, wu_ref, wrow_ref, o_ref):
    @pl.when(_moe_tile_is_used(te_ref))
    def _():
        x = x_ref[...]
        h = _silu(_dot(x, wg_ref[...])) * _dot(x, wu_ref[...])
        o_ref[...] = (h * wrow_ref[...]).astype(BF16)

    @pl.when(jnp.logical_not(_moe_tile_is_used(te_ref)))
    def _():
        o_ref[...] = jnp.zeros_like(o_ref)


def _moe_up(xs, wg, wu, wrow, tile_expert, *, tn=896):
    n_rows, k = xs.shape
    n_e = wg.shape[-1]
    w_spec = pl.BlockSpec((None, k, tn), lambda j, i, te: (te[i], 0, j))
    return pl.pallas_call(
        _moe_up_kernel,
        grid_spec=pltpu.PrefetchScalarGridSpec(
            num_scalar_prefetch=1,
            grid=(n_e // tn, n_rows // MOE_TILE),
            in_specs=[pl.BlockSpec((MOE_TILE, k), lambda j, i, te: (i, 0)), w_spec, w_spec,
                      pl.BlockSpec((MOE_TILE, 1), lambda j, i, te: (i, 0))],
            out_specs=pl.BlockSpec((MOE_TILE, tn), lambda j, i, te: (i, j))),
        out_shape=jax.ShapeDtypeStruct((n_rows, n_e), BF16),
        compiler_params=_cparams("arbitrary", "arbitrary"),
        name="moe_up",
    )(tile_expert, xs, wg, wu, wrow.reshape(n_rows, 1))


def _moe_down_kernel(te_ref, h_ref, w_ref, o_ref):
    @pl.when(_moe_tile_is_used(te_ref))
    def _():
        o_ref[...] = _dot(h_ref[...], w_ref[...])

    @pl.when(jnp.logical_not(_moe_tile_is_used(te_ref)))
    def _():
        o_ref[...] = jnp.zeros_like(o_ref)


def _moe_down(h, wd, tile_expert, *, tn=1024):
    n_rows, k = h.shape
    n = wd.shape[-1]
    return pl.pallas_call(
        _moe_down_kernel,
        grid_spec=pltpu.PrefetchScalarGridSpec(
            num_scalar_prefetch=1,
            grid=(n // tn, n_rows // MOE_TILE),
            in_specs=[pl.BlockSpec((MOE_TILE, k), lambda j, i, te: (i, 0)),
                      pl.BlockSpec((None, k, tn), lambda j, i, te: (te[i], 0, j))],
            out_specs=pl.BlockSpec((MOE_TILE, tn), lambda j, i, te: (i, j))),
        out_shape=jax.ShapeDtypeStruct((n_rows, n), F32),
        compiler_params=_cparams("arbitrary", "arbitrary"),
        name="moe_down",
    )(tile_expert, h, wd)


def _moe_combine_kernel(p1_ref, p2_ref, y_hbm, x_ref, w_ref, b_ref, o_ref, ob_ref, buf1, buf2, sem1, sem2, *, rows):
    def issue(r, c):
        _row_copy(y_hbm, p1_ref[0, r], buf1, r, sem1).start()
        _row_copy(y_hbm, p2_ref[0, r], buf2, r, sem2).start()
        return c

    lax.fori_loop(0, rows, issue, 0, unroll=8)
    pltpu.make_async_copy(y_hbm.at[pl.ds(0, rows), :], buf1, sem1).wait()
    pltpu.make_async_copy(y_hbm.at[pl.ds(0, rows), :], buf2, sem2).wait()
    x = DEEPNORM_ALPHA * x_ref[...] + (buf1[...] + buf2[...])
    mu = jnp.mean(x, axis=-1, keepdims=True)
    xc = x - mu
    var = jnp.mean(xc * xc, axis=-1, keepdims=True)
    y = xc * lax.rsqrt(var + 1e-5) * w_ref[...] + b_ref[...]
    o_ref[...] = y
    ob_ref[...] = y.astype(BF16)


def _moe_combine(ys, pos1, pos2, x, ln_w, ln_b, *, rows=128):
    t_all, d = x.shape
    rows = min(rows, t_all)
    idx = pl.BlockSpec((None, 1, rows), lambda i: (i, 0, 0), memory_space=pltpu.SMEM)
    row = pl.BlockSpec((rows, d), lambda i: (i, 0))
    vec = pl.BlockSpec((1, d), lambda i: (0, 0))
    return pl.pallas_call(
        functools.partial(_moe_combine_kernel, rows=rows),
        grid=(t_all // rows,),
        in_specs=[idx, idx, pl.BlockSpec(memory_space=pl.ANY), row, vec, vec],
        out_specs=[row, row],
        out_shape=[jax.ShapeDtypeStruct((t_all, d), F32), jax.ShapeDtypeStruct((t_all, d), BF16)],
        scratch_shapes=[pltpu.VMEM((rows, d), F32), pltpu.VMEM((rows, d), F32),
                        pltpu.SemaphoreType.DMA(()), pltpu.SemaphoreType.DMA(())],
        compiler_params=_cparams("arbitrary"),
        name="moe_combine",
    )(pos1.reshape(t_all // rows, 1, rows), pos2.reshape(t_all // rows, 1, rows), ys, x,
      ln_w.reshape(1, d), ln_b.reshape(1, d))


_CONV_HALO = SUBLANES


def _gdn_prep_kernel(x_ref, w_ref, o_ref, buf_ref, *, rows):
    @pl.when(pl.program_id(1) == 0)
    def _():
        buf_ref[0:_CONV_HALO, :] = jnp.zeros((_CONV_HALO, buf_ref.shape[1]), F32)

    buf_ref[_CONV_HALO:_CONV_HALO + rows, :] = x_ref[...].astype(F32)
    n_strips = x_ref.shape[1] // HEAD_DIM
    for s in range(n_strips):
        cols = slice(s * HEAD_DIM, (s + 1) * HEAD_DIM)
        acc = w_ref[GDN_CONV - 1:GDN_CONV, cols] * buf_ref[_CONV_HALO:_CONV_HALO + rows, cols]
        for i in range(GDN_CONV - 1):
            start = _CONV_HALO - (GDN_CONV - 1) + i
            acc = acc + w_ref[i:i + 1, cols] * buf_ref[start:start + rows, cols]
        y = _silu(acc)
        if s < 2 * N_HEADS:
            y = y * lax.rsqrt(jnp.sum(y * y, axis=-1, keepdims=True) + 1e-6)
            if s < N_HEADS:
                y = y * HEAD_DIM ** -0.5
        o_ref[:, cols] = y
    buf_ref[0:_CONV_HALO, :] = buf_ref[rows:rows + _CONV_HALO, :]


def _gdn_prep(proj, conv_w, *, batch, seq, rows=256):
    rows = min(rows, seq)
    width = 3 * GROUP_WIDTH
    nblk = seq // rows
    return pl.pallas_call(
        functools.partial(_gdn_prep_kernel, rows=rows),
        grid=(batch, nblk),
        in_specs=[pl.BlockSpec((rows, width), lambda b, s: (b * nblk + s, OFF_GDN_QKV // width)),
                  pl.BlockSpec((GDN_CONV, width), lambda b, s: (0, 0))],
        out_specs=pl.BlockSpec((rows, width), lambda b, s: (b * nblk + s, 0)),
        out_shape=jax.ShapeDtypeStruct((batch * seq, width), F32),
        scratch_shapes=[pltpu.VMEM((rows + _CONV_HALO, width), F32)],
        compiler_params=_cparams("parallel", "arbitrary"),
        name="gdn_prep",
    )(proj, conv_w)


def _split2(x):
    hi = x.astype(BF16)
    return hi, (x - hi.astype(F32)).astype(BF16)


def _split3(x):
    hi = x.astype(BF16)
    r = x - hi.astype(F32)
    mid = r.astype(BF16)
    return hi, mid, (r - mid.astype(F32)).astype(BF16)


def _dot_x3(a, b):
    a_hi, a_lo = _split2(a)
    b_hi, b_lo = _split2(b)
    return _dot(a_hi, b_hi) + (_dot(a_hi, b_lo) + _dot(a_lo, b_hi))


def _gdn_kernel(q_ref, k_ref, v_ref, z_ref, misc_ref, alog_ref, dtb_ref, nw_ref, o_ref, s_ref, *, n_chunks, hg):
    c_len = GDN_CHUNK
    h0 = pl.program_id(1) * hg

    @pl.when(pl.program_id(2) == 0)
    def _():
        s_ref[...] = jnp.zeros_like(s_ref)

    r_io = lax.broadcasted_iota(jnp.int32, (c_len, c_len), 0)
    c_io = lax.broadcasted_iota(jnp.int32, (c_len, c_len), 1)
    incl = r_io >= c_io
    strict = r_io > c_io
    l_incl = incl.astype(BF16)
    eye = (r_io == c_io).astype(F32)
    sel = lax.broadcasted_iota(jnp.int32, (LANES, LANES), 0)
    lane0 = (lax.broadcasted_iota(jnp.int32, (c_len, LANES), 1) == 0).astype(BF16)
    heads = range(hg)
    chunks = range(n_chunks)
    pairs = [(hh, c) for hh in heads for c in chunks]
    rows = [pl.ds(c * c_len, c_len) for c in chunks]
    cols = [slice(hh * HEAD_DIM, (hh + 1) * HEAD_DIM) for hh in heads]
    pick_a = [(sel == MISC_A + h0 + hh).astype(BF16) for hh in heads]
    pick_b = [(sel == MISC_B + h0 + hh).astype(BF16) for hh in heads]
    neg_a = [-jnp.exp(alog_ref[hh]) for hh in heads]
    misc = [misc_ref[rows[c], :] for c in chunks]

    def softplus(t):
        return jnp.maximum(t, 0.0) + jnp.log1p(jnp.exp(-jnp.abs(t)))

    g_b = {p: neg_a[p[0]] * softplus(_dot(misc[p[1]], pick_a[p[0]]) + dtb_ref[p[0]]) for p in pairs}
    beta_b = {p: jax.nn.sigmoid(_dot(misc[p[1]], pick_b[p[0]])) for p in pairs}
    gc_b = {p: sum(_dot(l_incl, part) for part in _split3(g_b[p])) for p in pairs}
    gc_row = {p: sum(_dot_nt(lane0, part) for part in _split3(gc_b[p])) for p in pairs}
    decay = {p: jnp.where(incl, jnp.exp(gc_b[p][:, :c_len] - gc_row[p]), 0.0) for p in pairs}

    q = {p: q_ref[rows[p[1]], cols[p[0]]] for p in pairs}
    k = {p: k_ref[rows[p[1]], cols[p[0]]] for p in pairs}
    k_bf = {p: k[p].astype(BF16) for p in pairs}
    kb = {p: k[p] * beta_b[p] for p in pairs}
    a_mat = {p: jnp.where(strict, _dot_nt(kb[p].astype(BF16), k_bf[p]) * decay[p], 0.0) for p in pairs}
    t_inv = {p: eye - a_mat[p] for p in pairs}
    a_pow = {p: _dot_x3(a_mat[p], a_mat[p]) for p in pairs}
    n_doublings = int(math.log2(c_len)) - 1
    for it in range(n_doublings):
        t_inv = {p: t_inv[p] + _dot_x3(t_inv[p], a_pow[p]) for p in pairs}
        if it < n_doublings - 1:
            a_pow = {p: _dot_x3(a_pow[p], a_pow[p]) for p in pairs}
    eg = {p: jnp.exp(gc_b[p]) for p in pairs}
    sol = {p: _dot_x3(t_inv[p], jnp.concatenate([kb[p] * eg[p], v_ref[rows[p[1]], cols[p[0]]] * beta_b[p]], axis=1))
           for p in pairs}
    w_bf = {p: sol[p][:, :HEAD_DIM].astype(BF16) for p in pairs}
    qk_bf = {p: (_dot_nt(q[p].astype(BF16), k_bf[p]) * decay[p]).astype(BF16) for p in pairs}
    qd_bf = {p: (q[p] * eg[p]).astype(BF16) for p in pairs}
    g_last = {p: gc_b[p][c_len - 1:c_len, :] for p in pairs}
    kd_bf = {p: (k[p] * jnp.exp(g_last[p] - gc_b[p])).astype(BF16) for p in pairs}
    eg_last = {p: jnp.exp(g_last[p]) for p in pairs}

    state = [s_ref[hh] for hh in heads]
    for c in chunks:
        state_bf = [state[hh].astype(BF16) for hh in heads]
        v_new = [sol[hh, c][:, HEAD_DIM:] - _dot(w_bf[hh, c], state_bf[hh]) for hh in heads]
        v_new_bf = [t.astype(BF16) for t in v_new]
        out = [_dot(qd_bf[hh, c], state_bf[hh]) + _dot(qk_bf[hh, c], v_new_bf[hh]) for hh in heads]
        state = [state[hh] * eg_last[hh, c] + _dot_tn(kd_bf[hh, c], v_new_bf[hh]) for hh in heads]
        for hh in heads:
            y = out[hh] * lax.rsqrt(jnp.mean(out[hh] * out[hh], axis=-1, keepdims=True) + 1e-6) * nw_ref[...]
            o_ref[rows[c], cols[hh]] = (y * _silu(z_ref[rows[c], cols[hh]].astype(F32))).astype(BF16)
    for hh in heads:
        s_ref[hh] = state[hh]


def _gdn(qkv, proj, a_log, dt_bias, norm_w, *, batch, seq, rows=256, hg=4):
    rows = min(rows, seq)
    nblk = seq // rows
    width = hg * HEAD_DIM
    groups = N_HEADS // hg

    def head_spec(col0):
        return pl.BlockSpec((rows, width), lambda b, g, s: (b * nblk + s, col0 // hg + g))

    per_head = pl.BlockSpec((hg, 1, LANES), lambda b, g, s: (g, 0, 0))
    bcast = lambda t: jnp.broadcast_to(t.astype(F32)[:, None, None], (N_HEADS, 1, LANES))
    return pl.pallas_call(
        functools.partial(_gdn_kernel, n_chunks=rows // GDN_CHUNK, hg=hg),
        grid=(batch, groups, nblk),
        in_specs=[head_spec(0), head_spec(N_HEADS), head_spec(2 * N_HEADS),
                  head_spec(OFF_GDN_Z // HEAD_DIM),
                  pl.BlockSpec((rows, LANES), lambda b, g, s: (b * nblk + s, OFF_MISC // LANES)),
                  per_head, per_head,
                  pl.BlockSpec((1, HEAD_DIM), lambda b, g, s: (0, 0))],
        out_specs=head_spec(0),
        out_shape=jax.ShapeDtypeStruct((batch * seq, GROUP_WIDTH), BF16),
        scratch_shapes=[pltpu.VMEM((hg, HEAD_DIM, HEAD_DIM), F32)],
        compiler_params=_cparams("parallel", "parallel", "arbitrary"),
        name="gdn",
    )(qkv, qkv, qkv, proj, proj, bcast(a_log), bcast(dt_bias), norm_w.reshape(1, HEAD_DIM))


_MLA_QK_PAD = 2 * HEAD_DIM
_MLA_IN = MLA_Q_RANK + LANES + MLA_KV_RANK


def _mla_prep_kernel(x_ref, qnw_ref, kvnw_ref, wuq_ref, wukv_ref, cos_ref, sin_ref, q_ref, k_ref, v_ref):
    x = x_ref[...]
    cq = x[:, :MLA_Q_RANK].astype(F32)
    misc = x[:, MLA_Q_RANK:MLA_Q_RANK + LANES].astype(F32)
    ckv = x[:, MLA_Q_RANK + LANES:].astype(F32)

    def rms(t, w):
        return t * lax.rsqrt(jnp.mean(t * t, axis=-1, keepdims=True) + 1e-6) * w

    mq = _dot(rms(cq, qnw_ref[...]).astype(BF16), wuq_ref[...])
    mkv = _dot(rms(ckv, kvnw_ref[...]).astype(BF16), wukv_ref[...])
    cos = cos_ref[...]
    sin = sin_ref[...]
    lane = lax.broadcasted_iota(jnp.int32, cos.shape, 1)
    first_half = (lane % MLA_ROPE_DIM) < MLA_ROPE_DIM // 2
    low = lane < MLA_ROPE_DIM

    def rope(t):
        swapped = jnp.where(first_half, pltpu.roll(t, LANES - MLA_ROPE_DIM // 2, 1),
                            pltpu.roll(t, MLA_ROPE_DIM // 2, 1))
        return t * cos + swapped * sin

    scale = (MLA_NOPE_DIM + MLA_ROPE_DIM) ** -0.5 * math.log2(math.e)
    k_rope = jnp.where(low, rope(misc), 0.0).astype(BF16)
    nope_w = N_HEADS * MLA_NOPE_DIM
    for h in range(N_HEADS):
        pair = rope(mq[:, nope_w + (h // 2) * LANES: nope_w + (h // 2 + 1) * LANES])
        q_rope = pair if h % 2 == 0 else pltpu.roll(pair, MLA_ROPE_DIM, 1)
        q_rope = jnp.where(low, q_rope, 0.0)
        base = h * _MLA_QK_PAD
        q_ref[:, base:base + HEAD_DIM] = (mq[:, h * HEAD_DIM:(h + 1) * HEAD_DIM] * scale).astype(BF16)
        q_ref[:, base + HEAD_DIM:base + _MLA_QK_PAD] = (q_rope * scale).astype(BF16)
        k_ref[:, base:base + HEAD_DIM] = mkv[:, h * HEAD_DIM:(h + 1) * HEAD_DIM].astype(BF16)
        k_ref[:, base + HEAD_DIM:base + _MLA_QK_PAD] = k_rope
    v_ref[...] = mkv[:, nope_w:].T.astype(BF16)


def _mla_prep(proj, q_norm_w, kv_norm_w, w_uq, w_ukv, cos, sin, *, seq, tm=512):
    t_all = proj.shape[0]
    tm = min(tm, seq)
    nblk = seq // tm
    full = lambda a: pl.BlockSpec(a.shape, lambda i: (0,) * a.ndim)
    table = pl.BlockSpec((tm, LANES), lambda i: (i % nblk, 0))
    qnw = q_norm_w.reshape(1, -1)
    kvnw = kv_norm_w.reshape(1, -1)
    wide = N_HEADS * _MLA_QK_PAD
    return pl.pallas_call(
        _mla_prep_kernel,
        grid=(t_all // tm,),
        in_specs=[pl.BlockSpec((tm, _MLA_IN), lambda i: (i, OFF_MLA // _MLA_IN)),
                  full(qnw), full(kvnw), full(w_uq), full(w_ukv), table, table],
        out_specs=[pl.BlockSpec((tm, wide), lambda i: (i, 0)),
                   pl.BlockSpec((tm, wide), lambda i: (i, 0)),
                   pl.BlockSpec((GROUP_WIDTH, tm), lambda i: (0, i))],
        out_shape=[jax.ShapeDtypeStruct((t_all, wide), BF16),
                   jax.ShapeDtypeStruct((t_all, wide), BF16),
                   jax.ShapeDtypeStruct((GROUP_WIDTH, t_all), BF16)],
        compiler_params=_cparams("parallel"),
        name="mla_prep",
    )(proj, qnw, kvnw, w_uq, w_ukv, cos, sin)


def _mla_attn_kernel(q_ref, k_ref, vt_ref, o_ref, m_ref, l_ref, acc_ref, *, t):
    i = pl.program_id(2)
    m_ref[...] = jnp.full_like(m_ref, -jnp.inf)
    l_ref[...] = jnp.zeros_like(l_ref)
    acc_ref[...] = jnp.zeros_like(acc_ref)

    def tile(kb, diagonal):
        start = pl.multiple_of(kb * t, t)
        s = _dot_nt(k_ref[pl.ds(start, t), :], q_ref[...])
        if diagonal:
            key = lax.broadcasted_iota(jnp.int32, s.shape, 0)
            qry = lax.broadcasted_iota(jnp.int32, s.shape, 1)
            s = jnp.where(key <= qry, s, -jnp.inf)
        m_prev = m_ref[...]
        m_new = jnp.maximum(m_prev, jnp.max(s, axis=0, keepdims=True))
        p = jnp.exp2(s - m_new)
        corr = jnp.exp2(m_prev - m_new)
        l_ref[...] = corr * l_ref[...] + jnp.sum(p, axis=0, keepdims=True)
        acc_ref[...] = corr * acc_ref[...] + _dot(vt_ref[:, pl.ds(start, t)], p.astype(BF16))
        m_ref[...] = m_new

    def body(kb, carry):
        tile(kb, False)
        return carry

    lax.fori_loop(0, i, body, 0)
    tile(i, True)
    o_ref[...] = (acc_ref[...] / l_ref[...]).T.astype(o_ref.dtype)


def _mla_attn(q, k, v, *, batch, seq, t=512):
    t = min(t, seq)
    nblk = seq // t
    return pl.pallas_call(
        functools.partial(_mla_attn_kernel, t=t),
        grid=(batch, N_HEADS, nblk),
        in_specs=[pl.BlockSpec((t, _MLA_QK_PAD), lambda b, h, i: (b * nblk + i, h)),
                  pl.BlockSpec((seq, _MLA_QK_PAD), lambda b, h, i: (b, h)),
                  pl.BlockSpec((HEAD_DIM, seq), lambda b, h, i: (h, b))],
        out_specs=pl.BlockSpec((t, HEAD_DIM), lambda b, h, i: (b * nblk + i, h)),
        out_shape=jax.ShapeDtypeStruct((batch * seq, GROUP_WIDTH), BF16),
        scratch_shapes=[pltpu.VMEM((1, t), F32), pltpu.VMEM((1, t), F32), pltpu.VMEM((HEAD_DIM, t), F32)],
        compiler_params=_cparams("parallel", "parallel", "arbitrary"),
        name="mla_attn",
    )(q, k, v)


def _sgu_kernel(x_ref, nw_ref, nb_ref, ws_ref, bs_ref, o_ref, *, n_chunks):
    x = x_ref[...].astype(F32)
    ge = 0.5 * x * (1.0 + lax.erf(x * np.float32(math.sqrt(0.5))))
    u = ge[:, :GROUP_WIDTH]
    sv = ge[:, GROUP_WIDTH:]
    mu = jnp.mean(sv, axis=-1, keepdims=True)
    svc = sv - mu
    var = jnp.mean(svc * svc, axis=-1, keepdims=True)
    svn = (svc * lax.rsqrt(var + 1e-5) * nw_ref[...] + nb_ref[...]).astype(BF16)
    r_io = lax.broadcasted_iota(jnp.int32, (SGU_CHUNK, SGU_CHUNK), 0)
    c_io = lax.broadcasted_iota(jnp.int32, (SGU_CHUNK, SGU_CHUNK), 1)
    causal = r_io >= c_io
    gd = GROUP_WIDTH // SGU_GROUPS
    for g in range(SGU_GROUPS):
        w_mix = jnp.where(causal, ws_ref[g], 0.0).astype(BF16)
        bias = bs_ref[:, g:g + 1]
        cols = slice(g * gd, (g + 1) * gd)
        for c in range(n_chunks):
            rows = slice(c * SGU_CHUNK, (c + 1) * SGU_CHUNK)
            mixed = _dot(w_mix, svn[rows, cols]) + bias
            o_ref[rows, cols] = (u[rows, cols] * mixed).astype(BF16)


def _sgu(proj, norm_w, norm_b, w_s, b_s, *, rows=256):
    t_all = proj.shape[0]
    rows = min(rows, t_all)
    width = 2 * GROUP_WIDTH
    bs_t = jnp.pad(b_s.T, ((0, 0), (0, LANES - SGU_GROUPS)))
    full = lambda a: pl.BlockSpec(a.shape, lambda i: (0,) * a.ndim)
    nw = norm_w.reshape(1, -1)
    nb = norm_b.reshape(1, -1)
    return pl.pallas_call(
        functools.partial(_sgu_kernel, n_chunks=rows // SGU_CHUNK),
        grid=(t_all // rows,),
        in_specs=[pl.BlockSpec((rows, width), lambda i: (i, OFF_SGU // width)),
                  full(nw), full(nb), full(w_s), full(bs_t)],
        out_specs=pl.BlockSpec((rows, GROUP_WIDTH), lambda i: (i, 0)),
        out_shape=jax.ShapeDtypeStruct((t_all, GROUP_WIDTH), BF16),
        compiler_params=_cparams("parallel"),
        name="sgu",
    )(proj, nw, nb, w_s, bs_t)


_SB_SUB = 256
_SB_ROW_SPLIT = 2


def _sb_kernel(q_ref, k_ref, v_ref, o_ref, carry_ref, acc_ref, *, t):
    i = pl.program_id(2)
    sub = min(_SB_SUB, t)
    carry_ref[...] = jnp.zeros_like(carry_ref)
    acc_ref[...] = jnp.zeros_like(acc_ref)
    sr = lax.broadcasted_iota(jnp.int32, (2 * sub, sub), 0)
    sc = lax.broadcasted_iota(jnp.int32, (2 * sub, sub), 1)
    suffix2 = (jnp.where(sr >= sub, sr - sub, sr) > sc).astype(BF16)

    n_half = max(1, min(_SB_ROW_SPLIT, t // LANES))
    hr = t // n_half
    halves = range(n_half)
    rsl = [slice(h * hr, (h + 1) * hr) for h in halves]

    def tile(kb, diagonal):
        start = pl.multiple_of(kb * t, t)
        k = k_ref[pl.ds(start, t), :]
        z = [_dot_nt(q_ref[rsl[h], :], k) * np.float32(HEAD_DIM ** -0.5) for h in halves]
        log_beta = [jnp.minimum(zz, 0.0) - jnp.log(1.0 + jnp.exp(-jnp.abs(zz))) for zz in z]
        log_rest = [log_beta[h] - z[h] for h in halves]
        if diagonal:
            strict = [lax.broadcasted_iota(jnp.int32, (hr, t), 1) < lax.broadcasted_iota(jnp.int32, (hr, t), 0) + h * hr
                      for h in halves]
            log_rest = [jnp.where(strict[h], log_rest[h], 0.0) for h in halves]
        carry = [carry_ref[rsl[h], :] for h in halves]
        acc = [acc_ref[rsl[h], :] for h in halves]
        for sb in reversed(range(t // sub)):
            cs = slice(sb * sub, (sb + 1) * sub)
            lr = [log_rest[h][:, cs] for h in halves]
            parts = [_split2(x) for x in lr]
            tail = [_dot(jnp.concatenate(parts[h], axis=1), suffix2) for h in halves]
            a = [jnp.exp(log_beta[h][:, cs] + (tail[h] + carry[h])) for h in halves]
            if diagonal:
                a = [jnp.where(strict[h][:, cs], a[h], 0.0) for h in halves]
            v = v_ref[pl.ds(start + sb * sub, sub), :]
            acc = [acc[h] + _dot(a[h].astype(BF16), v) for h in halves]
            carry = [carry[h] + (tail[h][:, 0:1] + lr[h][:, 0:1]) for h in halves]
        for h in halves:
            carry_ref[rsl[h], :] = carry[h]
            acc_ref[rsl[h], :] = acc[h]

    tile(i, True)

    def body(jj, c):
        tile(i - 1 - jj, False)
        return c

    lax.fori_loop(0, i, body, 0)
    o_ref[...] = acc_ref[...].astype(o_ref.dtype)


def _sb_attn(proj, *, batch, seq, t=512):
    t = min(t, seq)
    nblk = seq // t
    c0 = OFF_SB // HEAD_DIM
    return pl.pallas_call(
        functools.partial(_sb_kernel, t=t),
        grid=(batch, N_HEADS, nblk),
        in_specs=[pl.BlockSpec((t, HEAD_DIM), lambda b, h, i: (b * nblk + i, c0 + h)),
                  pl.BlockSpec((seq, HEAD_DIM), lambda b, h, i: (b, c0 + N_HEADS + h)),
                  pl.BlockSpec((seq, HEAD_DIM), lambda b, h, i: (b, c0 + 2 * N_HEADS + h))],
        out_specs=pl.BlockSpec((t, HEAD_DIM), lambda b, h, i: (b * nblk + i, h)),
        out_shape=jax.ShapeDtypeStruct((batch * seq, GROUP_WIDTH), BF16),
        scratch_shapes=[pltpu.VMEM((t, 1), F32), pltpu.VMEM((t, HEAD_DIM), F32)],
        compiler_params=_cparams("parallel", "parallel", "arbitrary"),
        name="sb_attn",
    )(proj, proj, proj)


def _cast_pad_kernel(x_ref, o_ref, *, nr, nc):
    inside = (pl.program_id(0) < nr) & (pl.program_id(1) < nc)

    @pl.when(inside)
    def _():
        o_ref[...] = x_ref[...].astype(BF16)

    @pl.when(jnp.logical_not(inside))
    def _():
        o_ref[...] = jnp.zeros_like(o_ref)


def _cast_pad(w_stack, index, *, out_rows=None, out_cols=None, tr, tc, name):
    _, r, c = w_stack.shape
    out_rows = out_rows or r
    out_cols = out_cols or c
    assert r % tr == 0 and c % tc == 0 and out_rows % tr == 0 and out_cols % tc == 0
    nr, nc = r // tr, c // tc
    return pl.pallas_call(
        functools.partial(_cast_pad_kernel, nr=nr, nc=nc),
        grid=(out_rows // tr, out_cols // tc),
        in_specs=[pl.BlockSpec((None, tr, tc), lambda i, j: (index, jnp.minimum(i, nr - 1), jnp.minimum(j, nc - 1)))],
        out_specs=pl.BlockSpec((tr, tc), lambda i, j: (i, j)),
        out_shape=jax.ShapeDtypeStruct((out_rows, out_cols), BF16),
        compiler_params=_cparams("parallel", "parallel"),
        name=name,
    )(w_stack)


_W_IN_MOVES = ((0, OFF_GDN_QKV, 4096), (5584, OFF_SGU, 2048), (7632, OFF_SB, 3072), (4112, OFF_MLA, MLA_Q_RANK),
               (5008, OFF_MISC + LANES, MLA_KV_RANK))
_W_IN_MISC = ((5520, MLA_ROPE_DIM), (4096, N_HEADS), (4104, N_HEADS))


def _w_in_kernel(x_ref, o_ref):
    for src, dst, width in _W_IN_MOVES:
        o_ref[:, dst:dst + width] = x_ref[:, src:src + width].astype(BF16)
    parts = [x_ref[:, src:src + width] for src, width in _W_IN_MISC]
    used = sum(width for _, width in _W_IN_MISC)
    parts.append(jnp.zeros((x_ref.shape[0], LANES - used), F32))
    o_ref[:, OFF_MISC:OFF_MISC + LANES] = jnp.concatenate(parts, axis=1).astype(BF16)


def _w_in_bf16(w_in, layer, *, tr=128):
    _, d, n = w_in.shape
    return pl.pallas_call(
        _w_in_kernel,
        grid=(d // tr,),
        in_specs=[pl.BlockSpec((None, tr, n), lambda i: (layer, i, 0))],
        out_specs=pl.BlockSpec((tr, D_IN_PAD), lambda i: (i, 0)),
        out_shape=jax.ShapeDtypeStruct((d, D_IN_PAD), BF16),
        compiler_params=_cparams("parallel"),
        name="w_in_prep",
    )(w_in)


def _split_heads_cols(w, first):
    k = w.shape[0]
    w3 = w.reshape(k, N_HEADS, -1)
    return jnp.concatenate([w3[:, :, :first].reshape(k, -1), w3[:, :, first:].reshape(k, -1)], axis=1)


def _rope_tables(seq):
    half = MLA_ROPE_DIM // 2
    pos = jnp.arange(seq, dtype=F32)
    inv_freq = ROPE_THETA ** (-jnp.arange(half, dtype=F32) / half)
    ang = pos[:, None] * inv_freq[None, :]
    cos, sin = jnp.cos(ang), jnp.sin(ang)
    reps = LANES // half
    sign = jnp.tile(jnp.concatenate([-jnp.ones((half,), F32), jnp.ones((half,), F32)]), LANES // MLA_ROPE_DIM)
    return jnp.tile(cos, (1, reps)), jnp.tile(sin, (1, reps)) * sign[None, :]


def kernel(x, w_in, gdn_conv_w, gdn_a_log, gdn_dt_bias, gdn_norm_w, mla_q_norm_w, mla_w_uq, mla_kv_norm_w, mla_w_ukv, sgu_norm_w, sgu_norm_b, sgu_w_s, sgu_b_s, w_out, ln_mix_w, ln_mix_b, ffn_w_gate, ffn_w_up, ffn_w_down, moe_router_w, moe_router_b, moe_w_gate, moe_w_up, moe_w_down, ln_ffn_w, ln_ffn_b):
    batch, seq, d = x.shape
    t_all = batch * seq
    cos, sin = _rope_tables(seq)
    xf = x.reshape(t_all, d)
    xb = xf.astype(BF16)
    for layer in range(DEPTH):
        proj = _matmul([xb], _w_in_bf16(w_in, layer), tm=1024, tn=768, tk=d, out_dtype=BF16, name="in_proj")
        qkv = _gdn_prep(proj, gdn_conv_w[layer], batch=batch, seq=seq)
        o_a = _gdn(qkv, proj, gdn_a_log[layer], gdn_dt_bias[layer], gdn_norm_w[layer], batch=batch, seq=seq)
        mq, mk, mv = _mla_prep(proj, mla_q_norm_w[layer], mla_kv_norm_w[layer],
                               _split_heads_cols(mla_w_uq[layer], MLA_NOPE_DIM).astype(BF16),
                               _split_heads_cols(mla_w_ukv[layer], MLA_NOPE_DIM).astype(BF16),
                               cos, sin, seq=seq)
        o_b = _mla_attn(mq, mk, mv, batch=batch, seq=seq)
        o_c = _sgu(proj, sgu_norm_w[layer], sgu_norm_b[layer], sgu_w_s[layer], sgu_b_s[layer])
        o_d = _sb_attn(proj, batch=batch, seq=seq)
        w_out_bf = _cast_pad(w_out, layer, tr=512, tc=2048, name="w_out_prep")
        pre = _matmul([o_a, o_b, o_c, o_d], w_out_bf, tm=1024, tn=512, tk=GROUP_WIDTH,
                      out_dtype=F32, res=xf, alpha=DEEPNORM_ALPHA, name="out_proj")
        xf, xb = _layer_norm(pre, ln_mix_w[layer], ln_mix_b[layer], name="ln_mix")
        i = layer // 2
        if layer % 2 == 0:
            wg = _cast_pad(ffn_w_gate, i, out_cols=D_FF_PAD, tr=2048, tc=256, name="ffn_w_prep")
            wu = _cast_pad(ffn_w_up, i, out_cols=D_FF_PAD, tr=2048, tc=256, name="ffn_w_prep")
            wd = _cast_pad(ffn_w_down, i, out_rows=D_FF_PAD, tr=256, tc=d, name="ffn_w_prep")
            hid = _gate_up(xb, wg, wu, tm=1024, tn=512, name="ffn_gate_up")
            pre = _matmul([hid], wd, tm=1024, tn=1024, tk=2816, out_dtype=F32, res=xf, alpha=DEEPNORM_ALPHA,
                          name="ffn_down")
            xf, xb = _layer_norm(pre, ln_ffn_w[layer], ln_ffn_b[layer], name="ln_ffn")
        else:
            meta, counts = _router(xf, moe_router_w[i], moe_router_b[i])
            pos1, pos2, src, wrow, tile_expert = _moe_plan(meta, counts, t_all)
            xs = _moe_gather(xf, src)
            n_moe = moe_w_gate.shape[0]
            e_rows = N_EXPERTS * d
            wg = _cast_pad(moe_w_gate.reshape(n_moe, e_rows, D_EXPERT), i, tr=1024, tc=896, name="moe_w_prep")
            wu = _cast_pad(moe_w_up.reshape(n_moe, e_rows, D_EXPERT), i, tr=1024, tc=896, name="moe_w_prep")
            wd = _cast_pad(moe_w_down.reshape(n_moe, N_EXPERTS * D_EXPERT, d), i, tr=512, tc=2048, name="moe_w_prep")
            hid = _moe_up(xs, wg.reshape(N_EXPERTS, d, D_EXPERT), wu.reshape(N_EXPERTS, d, D_EXPERT), wrow, tile_expert)
            ys = _moe_down(hid, wd.reshape(N_EXPERTS, D_EXPERT, d), tile_expert)
            xf, xb = _moe_combine(ys, pos1, pos2, xf, ln_ffn_w[layer], ln_ffn_b[layer])
    return xf.reshape(batch, seq, d)
```

```python
import functools
import math

import jax
import jax.numpy as jnp
import numpy as np
from jax import lax
from jax.experimental import pallas as pl
from jax.experimental.pallas import tpu as pltpu

F32 = jnp.float32
BF16 = jnp.bfloat16
HIGHEST = lax.Precision.HIGHEST

D_MODEL = 4096
DEPTH = 4
HEAD_DIM = 128
GROUP_WIDTH = 1024
N_HEADS = 8
GDN_CONV = 4
GDN_CHUNK = 64
MLA_Q_RANK = 896
MLA_KV_RANK = 512
MLA_NOPE_DIM = 128
MLA_ROPE_DIM = 64
ROPE_THETA = 10000.0
SGU_CHUNK = 128
SGU_GROUPS = 8
D_FF = 11008
N_EXPERTS = 8
D_EXPERT = 1792
DEEPNORM_ALPHA = (2 * DEPTH) ** 0.25

LANES = 128
SUBLANES = 8
VMEM_LIMIT_BYTES = 56 * 1024 * 1024

OFF_GDN_QKV = 0
OFF_GDN_Z = 3072
OFF_SGU = 4096
OFF_SB = 6144
OFF_MLA = 9216
OFF_MISC = OFF_MLA + MLA_Q_RANK
D_IN_PAD = 10752
MISC_A = 64
MISC_B = 72
D_FF_PAD = 11264


def _cparams(*sem):
    return pltpu.CompilerParams(dimension_semantics=sem, vmem_limit_bytes=VMEM_LIMIT_BYTES)


def _dot(a, b):
    return jnp.dot(a, b, preferred_element_type=F32)


def _dot_nt(a, b, precision=None):
    return lax.dot_general(a, b, (((1,), (1,)), ((), ())), preferred_element_type=F32, precision=precision)


def _dot_tn(a, b):
    return lax.dot_general(a, b, (((0,), (0,)), ((), ())), preferred_element_type=F32)


def _silu(x):
    return x * jax.nn.sigmoid(x)


def _mm_kernel(*refs, n_a, nk, alpha, has_res):
    a_refs = refs[:n_a]
    w_refs = refs[n_a:2 * n_a]
    pos = 2 * n_a
    res_ref = refs[pos] if has_res else None
    pos += int(has_res)
    o_ref = refs[pos]
    acc_ref = refs[pos + 1] if nk > 1 else None

    def partial_product():
        acc = _dot(a_refs[0][...], w_refs[0][...])
        for a_ref, w_ref in zip(a_refs[1:], w_refs[1:]):
            acc = acc + _dot(a_ref[...], w_ref[...])
        return acc

    def epilogue(acc):
        if has_res:
            acc = acc + alpha * res_ref[...]
        o_ref[...] = acc.astype(o_ref.dtype)

    if nk == 1:
        epilogue(partial_product())
    else:
        k = pl.program_id(2)

        @pl.when(k == 0)
        def _():
            acc_ref[...] = jnp.zeros_like(acc_ref)

        acc_ref[...] += partial_product()

        @pl.when(k == nk - 1)
        def _():
            epilogue(acc_ref[...])


def _matmul(a_list, w, *, tm, tn, tk, out_dtype, res=None, alpha=None, name="matmul"):
    n_a = len(a_list)
    m, k_seg = a_list[0].shape
    n = w.shape[-1]
    tm, tn, tk = min(tm, m), min(tn, n), min(tk, k_seg)
    nk = k_seg // tk
    assert m % tm == 0 and n % tn == 0 and k_seg % tk == 0
    assert n_a == 1 or nk == 1
    in_specs = [pl.BlockSpec((tm, tk), lambda i, j, k: (i, k)) for _ in a_list]
    in_specs += [pl.BlockSpec((tk, tn), functools.partial(lambda i, j, k, s: (s * nk + k, j), s=s))
                 for s in range(n_a)]
    operands = list(a_list) + [w] * n_a
    if res is not None:
        in_specs.append(pl.BlockSpec((tm, tn), lambda i, j, k: (i, j)))
        operands.append(res)
    return pl.pallas_call(
        functools.partial(_mm_kernel, n_a=n_a, nk=nk, alpha=alpha, has_res=res is not None),
        grid=(m // tm, n // tn, nk),
        in_specs=in_specs,
        out_specs=pl.BlockSpec((tm, tn), lambda i, j, k: (i, j)),
        out_shape=jax.ShapeDtypeStruct((m, n), out_dtype),
        scratch_shapes=[pltpu.VMEM((tm, tn), F32)] if nk > 1 else [],
        compiler_params=_cparams("parallel", "parallel", "arbitrary"),
        name=name,
    )(*operands)


def _gate_up_kernel(x_ref, wg_ref, wu_ref, o_ref):
    x = x_ref[...]
    o_ref[...] = (_silu(_dot(x, wg_ref[...])) * _dot(x, wu_ref[...])).astype(o_ref.dtype)


def _gate_up(x, wg, wu, *, tm, tn, name="gate_up"):
    m, k = x.shape
    n = wg.shape[-1]
    tm, tn = min(tm, m), min(tn, n)
    assert m % tm == 0 and n % tn == 0
    w_spec = pl.BlockSpec((k, tn), lambda i, j: (0, j))
    return pl.pallas_call(
        _gate_up_kernel,
        grid=(m // tm, n // tn),
        in_specs=[pl.BlockSpec((tm, k), lambda i, j: (i, 0)), w_spec, w_spec],
        out_specs=pl.BlockSpec((tm, tn), lambda i, j: (i, j)),
        out_shape=jax.ShapeDtypeStruct((m, n), BF16),
        compiler_params=_cparams("parallel", "arbitrary"),
        name=name,
    )(x, wg, wu)


def _ln_kernel(x_ref, w_ref, b_ref, o_ref, ob_ref):
    x = x_ref[...]
    mu = jnp.mean(x, axis=-1, keepdims=True)
    xc = x - mu
    var = jnp.mean(xc * xc, axis=-1, keepdims=True)
    y = xc * lax.rsqrt(var + 1e-5) * w_ref[...] + b_ref[...]
    o_ref[...] = y
    ob_ref[...] = y.astype(BF16)


def _layer_norm(x, w, b, *, tm=256, name="layer_norm"):
    m, d = x.shape
    tm = min(tm, m)
    row = pl.BlockSpec((tm, d), lambda i: (i, 0))
    vec = pl.BlockSpec((1, d), lambda i: (0, 0))
    return pl.pallas_call(
        _ln_kernel,
        grid=(m // tm,),
        in_specs=[row, vec, vec],
        out_specs=[row, row],
        out_shape=[jax.ShapeDtypeStruct((m, d), F32), jax.ShapeDtypeStruct((m, d), BF16)],
        compiler_params=_cparams("parallel"),
        name=name,
    )(x, w.reshape(1, d), b.reshape(1, d))


META_E1, META_E2, META_W1, META_W2, META_R1, META_R2 = range(6)


def _router_kernel(x_ref, w_ref, b_ref, meta_ref, counts_ref, cnt_ref):
    @pl.when(pl.program_id(0) == 0)
    def _():
        cnt_ref[...] = jnp.zeros_like(cnt_ref)

    logits = jnp.dot(x_ref[...], w_ref[...], preferred_element_type=F32, precision=HIGHEST) + b_ref[...]
    tm = logits.shape[0]
    lane = lax.broadcasted_iota(jnp.int32, logits.shape, 1)
    neg = jnp.float32(-jnp.inf)
    logits = jnp.where(lane < N_EXPERTS, logits, neg)
    m1 = jnp.max(logits, axis=1, keepdims=True)
    i1 = jnp.min(jnp.where(logits == m1, lane, LANES), axis=1, keepdims=True)
    rest = jnp.where(lane == i1, neg, logits)
    m2 = jnp.max(rest, axis=1, keepdims=True)
    i2 = jnp.min(jnp.where(rest == m2, lane, LANES), axis=1, keepdims=True)
    e2 = jnp.exp(m2 - m1)
    denom = 1.0 + e2
    chosen = (lane == i1) | (lane == i2)
    earlier = (lax.broadcasted_iota(jnp.int32, (tm, tm), 0) > lax.broadcasted_iota(jnp.int32, (tm, tm), 1))
    before = _dot(earlier.astype(BF16), chosen.astype(BF16)) + cnt_ref[...]
    rank1 = jnp.sum(jnp.where(lane == i1, before, 0.0), axis=1, keepdims=True)
    rank2 = jnp.sum(jnp.where(lane == i2, before, 0.0), axis=1, keepdims=True)
    cnt_ref[...] += jnp.sum(chosen.astype(F32), axis=0, keepdims=True)
    fields = {META_E1: i1.astype(F32), META_E2: i2.astype(F32), META_W1: 1.0 / denom, META_W2: e2 / denom,
              META_R1: rank1, META_R2: rank2}
    meta = jnp.zeros(logits.shape, F32)
    for pos, val in fields.items():
        meta = jnp.where(lane == pos, val, meta)
    meta_ref[...] = meta
    counts_ref[...] = cnt_ref[...]


def _router(x, w, b, *, tm=512, name="router"):
    m, d = x.shape
    tm = min(tm, m)
    w_pad = jnp.pad(w, ((0, 0), (0, LANES - N_EXPERTS)))
    b_pad = jnp.pad(b, (0, LANES - N_EXPERTS)).reshape(1, LANES)
    return pl.pallas_call(
        _router_kernel,
        grid=(m // tm,),
        in_specs=[pl.BlockSpec((tm, d), lambda i: (i, 0)),
                  pl.BlockSpec((d, LANES), lambda i: (0, 0)),
                  pl.BlockSpec((1, LANES), lambda i: (0, 0))],
        out_specs=[pl.BlockSpec((tm, LANES), lambda i: (i, 0)),
                   pl.BlockSpec((1, LANES), lambda i: (0, 0))],
        out_shape=[jax.ShapeDtypeStruct((m, LANES), F32), jax.ShapeDtypeStruct((1, LANES), F32)],
        scratch_shapes=[pltpu.VMEM((1, LANES), F32)],
        compiler_params=_cparams("arbitrary"),
        name=name,
    )(x, w_pad, b_pad)


MOE_TILE = 512


def _moe_plan(meta, counts, t_all):
    as_int = lambda lane: meta[:, lane].astype(jnp.int32)
    e1, e2, r1, r2 = as_int(META_E1), as_int(META_E2), as_int(META_R1), as_int(META_R2)
    cnt = counts[0, :N_EXPERTS].astype(jnp.int32)
    padded = (cnt + MOE_TILE - 1) // MOE_TILE * MOE_TILE
    ends = jnp.cumsum(padded)
    starts = ends - padded
    pos1 = starts[e1] + r1
    pos2 = starts[e2] + r2
    n_rows = 2 * t_all + N_EXPERTS * MOE_TILE
    tok = jnp.arange(t_all, dtype=jnp.int32)
    src = jnp.zeros((n_rows,), jnp.int32).at[pos1].set(tok).at[pos2].set(tok)
    wrow = jnp.zeros((n_rows,), F32).at[pos1].set(meta[:, META_W1]).at[pos2].set(meta[:, META_W2])
    tile_start = jnp.arange(n_rows // MOE_TILE, dtype=jnp.int32) * MOE_TILE
    tile_expert = jnp.minimum(jnp.searchsorted(ends, tile_start, side="right"), N_EXPERTS - 1).astype(jnp.int32)
    tile_expert = jnp.concatenate([tile_expert, (ends[-1:] // MOE_TILE).astype(jnp.int32)])
    return pos1, pos2, src, wrow, tile_expert


def _row_copy(src_hbm, row, dst_vmem, slot, sem):
    return pltpu.make_async_copy(src_hbm.at[pl.ds(row, 1), :], dst_vmem.at[pl.ds(slot, 1), :], sem)


def _moe_gather_kernel(src_ref, x_hbm, o_ref, buf_ref, sem, *, rows):
    def issue(r, c):
        _row_copy(x_hbm, src_ref[0, r], buf_ref, r, sem).start()
        return c

    lax.fori_loop(0, rows, issue, 0, unroll=8)
    pltpu.make_async_copy(x_hbm.at[pl.ds(0, rows), :], buf_ref, sem).wait()
    o_ref[...] = buf_ref[...].astype(BF16)


def _moe_gather(x, src, *, rows=512):
    n_rows = src.shape[0]
    d = x.shape[1]
    assert n_rows % rows == 0
    return pl.pallas_call(
        functools.partial(_moe_gather_kernel, rows=rows),
        grid=(n_rows // rows,),
        in_specs=[pl.BlockSpec((None, 1, rows), lambda i: (i, 0, 0), memory_space=pltpu.SMEM),
                  pl.BlockSpec(memory_space=pl.ANY)],
        out_specs=pl.BlockSpec((rows, d), lambda i: (i, 0)),
        out_shape=jax.ShapeDtypeStruct((n_rows, d), BF16),
        scratch_shapes=[pltpu.VMEM((rows, d), F32), pltpu.SemaphoreType.DMA(())],
        compiler_params=_cparams("arbitrary"),
        name="moe_gather",
    )(src.reshape(n_rows // rows, 1, rows), x)


def _moe_tile_is_used(te_ref):
    return pl.program_id(1) < te_ref[pl.num_programs(1)]


def _moe_up_kernel(te_ref, x_ref, wg_ref, wu_ref, wrow_ref, o_ref):
    @pl.when(_moe_tile_is_used(te_ref))
    def _():
        x = x_ref[...]
        h = _silu(_dot(x, wg_ref[...])) * _dot(x, wu_ref[...])
        o_ref[...] = (h * wrow_ref[...]).astype(BF16)

    @pl.when(jnp.logical_not(_moe_tile_is_used(te_ref)))
    def _():
        o_ref[...] = jnp.zeros_like(o_ref)


def _moe_up(xs, wg, wu, wrow, tile_expert, *, tn=896):
    n_rows, k = xs.shape
    n_e = wg.shape[-1]
    w_spec = pl.BlockSpec((None, k, tn), lambda j, i, te: (te[i], 0, j))
    return pl.pallas_call(
        _moe_up_kernel,
        grid_spec=pltpu.PrefetchScalarGridSpec(
            num_scalar_prefetch=1,
            grid=(n_e // tn, n_rows // MOE_TILE),
            in_specs=[pl.BlockSpec((MOE_TILE, k), lambda j, i, te: (i, 0)), w_spec, w_spec,
                      pl.BlockSpec((MOE_TILE, 1), lambda j, i, te: (i, 0))],
            out_specs=pl.BlockSpec((MOE_TILE, tn), lambda j, i, te: (i, j))),
        out_shape=jax.ShapeDtypeStruct((n_rows, n_e), BF16),
        compiler_params=_cparams("arbitrary", "arbitrary"),
        name="moe_up",
    )(tile_expert, xs, wg, wu, wrow.reshape(n_rows, 1))


def _moe_down_kernel(te_ref, h_ref, w_ref, o_ref):
    @pl.when(_moe_tile_is_used(te_ref))
    def _():
        o_ref[...] = _dot(h_ref[...], w_ref[...])

    @pl.when(jnp.logical_not(_moe_tile_is_used(te_ref)))
    def _():
        o_ref[...] = jnp.zeros_like(o_ref)


def _moe_down(h, wd, tile_expert, *, tn=1024):
    n_rows, k = h.shape
    n = wd.shape[-1]
    return pl.pallas_call(
        _moe_down_kernel,
        grid_spec=pltpu.PrefetchScalarGridSpec(
            num_scalar_prefetch=1,
            grid=(n // tn, n_rows // MOE_TILE),
            in_specs=[pl.BlockSpec((MOE_TILE, k), lambda j, i, te: (i, 0)),
                      pl.BlockSpec((None, k, tn), lambda j, i, te: (te[i], 0, j))],
            out_specs=pl.BlockSpec((MOE_TILE, tn), lambda j, i, te: (i, j))),
        out_shape=jax.ShapeDtypeStruct((n_rows, n), F32),
        compiler_params=_cparams("arbitrary", "arbitrary"),
        name="moe_down",
    )(tile_expert, h, wd)


def _moe_combine_kernel(p1_ref, p2_ref, y_hbm, x_ref, w_ref, b_ref, o_ref, ob_ref, buf1, buf2, sem1, sem2, *, rows):
    def issue(r, c):
        _row_copy(y_hbm, p1_ref[0, r], buf1, r, sem1).start()
        _row_copy(y_hbm, p2_ref[0, r], buf2, r, sem2).start()
        return c

    lax.fori_loop(0, rows, issue, 0, unroll=8)
    pltpu.make_async_copy(y_hbm.at[pl.ds(0, rows), :], buf1, sem1).wait()
    pltpu.make_async_copy(y_hbm.at[pl.ds(0, rows), :], buf2, sem2).wait()
    x = DEEPNORM_ALPHA * x_ref[...] + (buf1[...] + buf2[...])
    mu = jnp.mean(x, axis=-1, keepdims=True)
    xc = x - mu
    var = jnp.mean(xc * xc, axis=-1, keepdims=True)
    y = xc * lax.rsqrt(var + 1e-5) * w_ref[...] + b_ref[...]
    o_ref[...] = y
    ob_ref[...] = y.astype(BF16)


def _moe_combine(ys, pos1, pos2, x, ln_w, ln_b, *, rows=256):
    t_all, d = x.shape
    rows = min(rows, t_all)
    idx = pl.BlockSpec((None, 1, rows), lambda i: (i, 0, 0), memory_space=pltpu.SMEM)
    row = pl.BlockSpec((rows, d), lambda i: (i, 0))
    vec = pl.BlockSpec((1, d), lambda i: (0, 0))
    return pl.pallas_call(
        functools.partial(_moe_combine_kernel, rows=rows),
        grid=(t_all // rows,),
        in_specs=[idx, idx, pl.BlockSpec(memory_space=pl.ANY), row, vec, vec],
        out_specs=[row, row],
        out_shape=[jax.ShapeDtypeStruct((t_all, d), F32), jax.ShapeDtypeStruct((t_all, d), BF16)],
        scratch_shapes=[pltpu.VMEM((rows, d), F32), pltpu.VMEM((rows, d), F32),
                        pltpu.SemaphoreType.DMA(()), pltpu.SemaphoreType.DMA(())],
        compiler_params=_cparams("arbitrary"),
        name="moe_combine",
    )(pos1.reshape(t_all // rows, 1, rows), pos2.reshape(t_all // rows, 1, rows), ys, x,
      ln_w.reshape(1, d), ln_b.reshape(1, d))


_CONV_HALO = SUBLANES


def _gdn_prep_kernel(x_ref, w_ref, o_ref, buf_ref, *, rows):
    @pl.when(pl.program_id(1) == 0)
    def _():
        buf_ref[0:_CONV_HALO, :] = jnp.zeros((_CONV_HALO, buf_ref.shape[1]), F32)

    buf_ref[_CONV_HALO:_CONV_HALO + rows, :] = x_ref[...].astype(F32)
    n_strips = x_ref.shape[1] // HEAD_DIM
    for s in range(n_strips):
        cols = slice(s * HEAD_DIM, (s + 1) * HEAD_DIM)
        acc = w_ref[GDN_CONV - 1:GDN_CONV, cols] * buf_ref[_CONV_HALO:_CONV_HALO + rows, cols]
        for i in range(GDN_CONV - 1):
            start = _CONV_HALO - (GDN_CONV - 1) + i
            acc = acc + w_ref[i:i + 1, cols] * buf_ref[start:start + rows, cols]
        y = _silu(acc)
        if s < 2 * N_HEADS:
            y = y * lax.rsqrt(jnp.sum(y * y, axis=-1, keepdims=True) + 1e-6)
            if s < N_HEADS:
                y = y * HEAD_DIM ** -0.5
        o_ref[:, cols] = y
    buf_ref[0:_CONV_HALO, :] = buf_ref[rows:rows + _CONV_HALO, :]


def _gdn_prep(proj, conv_w, *, batch, seq, rows=256):
    rows = min(rows, seq)
    width = 3 * GROUP_WIDTH
    nblk = seq // rows
    return pl.pallas_call(
        functools.partial(_gdn_prep_kernel, rows=rows),
        grid=(batch, nblk),
        in_specs=[pl.BlockSpec((rows, width), lambda b, s: (b * nblk + s, OFF_GDN_QKV // width)),
                  pl.BlockSpec((GDN_CONV, width), lambda b, s: (0, 0))],
        out_specs=pl.BlockSpec((rows, width), lambda b, s: (b * nblk + s, 0)),
        out_shape=jax.ShapeDtypeStruct((batch * seq, width), F32),
        scratch_shapes=[pltpu.VMEM((rows + _CONV_HALO, width), F32)],
        compiler_params=_cparams("parallel", "arbitrary"),
        name="gdn_prep",
    )(proj, conv_w)


def _split2(x):
    hi = x.astype(BF16)
    return hi, (x - hi.astype(F32)).astype(BF16)


def _split3(x):
    hi = x.astype(BF16)
    r = x - hi.astype(F32)
    mid = r.astype(BF16)
    return hi, mid, (r - mid.astype(F32)).astype(BF16)


def _dot_x3(a, b):
    a_hi, a_lo = _split2(a)
    b_hi, b_lo = _split2(b)
    return _dot(a_hi, b_hi) + (_dot(a_hi, b_lo) + _dot(a_lo, b_hi))


def _gdn_kernel(q_ref, k_ref, v_ref, z_ref, misc_ref, alog_ref, dtb_ref, nw_ref, o_ref, s_ref, *, n_chunks, hg):
    c_len = GDN_CHUNK
    h0 = pl.program_id(1) * hg

    @pl.when(pl.program_id(2) == 0)
    def _():
        s_ref[...] = jnp.zeros_like(s_ref)

    r_io = lax.broadcasted_iota(jnp.int32, (c_len, c_len), 0)
    c_io = lax.broadcasted_iota(jnp.int32, (c_len, c_len), 1)
    incl = r_io >= c_io
    strict = r_io > c_io
    l_incl = incl.astype(BF16)
    eye = (r_io == c_io).astype(F32)
    sel = lax.broadcasted_iota(jnp.int32, (LANES, LANES), 0)
    lane0 = (lax.broadcasted_iota(jnp.int32, (c_len, LANES), 1) == 0).astype(BF16)
    heads = range(hg)
    chunks = range(n_chunks)
    pairs = [(hh, c) for hh in heads for c in chunks]
    rows = [pl.ds(c * c_len, c_len) for c in chunks]
    cols = [slice(hh * HEAD_DIM, (hh + 1) * HEAD_DIM) for hh in heads]
    pick_a = [(sel == MISC_A + h0 + hh).astype(BF16) for hh in heads]
    pick_b = [(sel == MISC_B + h0 + hh).astype(BF16) for hh in heads]
    neg_a = [-jnp.exp(alog_ref[hh]) for hh in heads]
    misc = [misc_ref[rows[c], :] for c in chunks]

    def softplus(t):
        return jnp.maximum(t, 0.0) + jnp.log1p(jnp.exp(-jnp.abs(t)))

    g_b = {p: neg_a[p[0]] * softplus(_dot(misc[p[1]], pick_a[p[0]]) + dtb_ref[p[0]]) for p in pairs}
    beta_b = {p: jax.nn.sigmoid(_dot(misc[p[1]], pick_b[p[0]])) for p in pairs}
    gc_b = {p: sum(_dot(l_incl, part) for part in _split3(g_b[p])) for p in pairs}
    gc_row = {p: sum(_dot_nt(lane0, part) for part in _split3(gc_b[p])) for p in pairs}
    decay = {p: jnp.where(incl, jnp.exp(gc_b[p][:, :c_len] - gc_row[p]), 0.0) for p in pairs}

    q = {p: q_ref[rows[p[1]], cols[p[0]]] for p in pairs}
    k = {p: k_ref[rows[p[1]], cols[p[0]]] for p in pairs}
    k_bf = {p: k[p].astype(BF16) for p in pairs}
    kb = {p: k[p] * beta_b[p] for p in pairs}
    a_mat = {p: jnp.where(strict, _dot_nt(kb[p].astype(BF16), k_bf[p]) * decay[p], 0.0) for p in pairs}
    t_inv = {p: eye - a_mat[p] for p in pairs}
    a_pow = {p: _dot_x3(a_mat[p], a_mat[p]) for p in pairs}
    n_doublings = int(math.log2(c_len)) - 1
    for it in range(n_doublings):
        t_inv = {p: t_inv[p] + _dot_x3(t_inv[p], a_pow[p]) for p in pairs}
        if it < n_doublings - 1:
            a_pow = {p: _dot_x3(a_pow[p], a_pow[p]) for p in pairs}
    eg = {p: jnp.exp(gc_b[p]) for p in pairs}
    sol = {p: _dot_x3(t_inv[p], jnp.concatenate([kb[p] * eg[p], v_ref[rows[p[1]], cols[p[0]]] * beta_b[p]], axis=1))
           for p in pairs}
    w_bf = {p: sol[p][:, :HEAD_DIM].astype(BF16) for p in pairs}
    qk_bf = {p: (_dot_nt(q[p].astype(BF16), k_bf[p]) * decay[p]).astype(BF16) for p in pairs}
    qd_bf = {p: (q[p] * eg[p]).astype(BF16) for p in pairs}
    g_last = {p: gc_b[p][c_len - 1:c_len, :] for p in pairs}
    kd_bf = {p: (k[p] * jnp.exp(g_last[p] - gc_b[p])).astype(BF16) for p in pairs}
    eg_last = {p: jnp.exp(g_last[p]) for p in pairs}

    state = [s_ref[hh] for hh in heads]
    for c in chunks:
        state_bf = [state[hh].astype(BF16) for hh in heads]
        v_new = [sol[hh, c][:, HEAD_DIM:] - _dot(w_bf[hh, c], state_bf[hh]) for hh in heads]
        v_new_bf = [t.astype(BF16) for t in v_new]
        out = [_dot(qd_bf[hh, c], state_bf[hh]) + _dot(qk_bf[hh, c], v_new_bf[hh]) for hh in heads]
        state = [state[hh] * eg_last[hh, c] + _dot_tn(kd_bf[hh, c], v_new_bf[hh]) for hh in heads]
        for hh in heads:
            y = out[hh] * lax.rsqrt(jnp.mean(out[hh] * out[hh], axis=-1, keepdims=True) + 1e-6) * nw_ref[...]
            o_ref[rows[c], cols[hh]] = (y * _silu(z_ref[rows[c], cols[hh]].astype(F32))).astype(BF16)
    for hh in heads:
        s_ref[hh] = state[hh]


def _gdn(qkv, proj, a_log, dt_bias, norm_w, *, batch, seq, rows=256, hg=8):
    rows = min(rows, seq)
    nblk = seq // rows
    width = hg * HEAD_DIM
    groups = N_HEADS // hg

    def head_spec(col0):
        return pl.BlockSpec((rows, width), lambda b, g, s: (b * nblk + s, col0 // hg + g))

    per_head = pl.BlockSpec((hg, 1, LANES), lambda b, g, s: (g, 0, 0))
    bcast = lambda t: jnp.broadcast_to(t.astype(F32)[:, None, None], (N_HEADS, 1, LANES))
    return pl.pallas_call(
        functools.partial(_gdn_kernel, n_chunks=rows // GDN_CHUNK, hg=hg),
        grid=(batch, groups, nblk),
        in_specs=[head_spec(0), head_spec(N_HEADS), head_spec(2 * N_HEADS),
                  head_spec(OFF_GDN_Z // HEAD_DIM),
                  pl.BlockSpec((rows, LANES), lambda b, g, s: (b * nblk + s, OFF_MISC // LANES)),
                  per_head, per_head,
                  pl.BlockSpec((1, HEAD_DIM), lambda b, g, s: (0, 0))],
        out_specs=head_spec(0),
        out_shape=jax.ShapeDtypeStruct((batch * seq, GROUP_WIDTH), BF16),
        scratch_shapes=[pltpu.VMEM((hg, HEAD_DIM, HEAD_DIM), F32)],
        compiler_params=_cparams("parallel", "parallel", "arbitrary"),
        name="gdn",
    )(qkv, qkv, qkv, proj, proj, bcast(a_log), bcast(dt_bias), norm_w.reshape(1, HEAD_DIM))


_MLA_QK_PAD = 2 * HEAD_DIM
_MLA_IN = MLA_Q_RANK + LANES + MLA_KV_RANK


def _mla_prep_kernel(x_ref, qnw_ref, kvnw_ref, wuq_ref, wukv_ref, cos_ref, sin_ref, q_ref, k_ref, v_ref):
    x = x_ref[...]
    cq = x[:, :MLA_Q_RANK].astype(F32)
    misc = x[:, MLA_Q_RANK:MLA_Q_RANK + LANES].astype(F32)
    ckv = x[:, MLA_Q_RANK + LANES:].astype(F32)

    def rms(t, w):
        return t * lax.rsqrt(jnp.mean(t * t, axis=-1, keepdims=True) + 1e-6) * w

    mq = _dot(rms(cq, qnw_ref[...]).astype(BF16), wuq_ref[...])
    mkv = _dot(rms(ckv, kvnw_ref[...]).astype(BF16), wukv_ref[...])
    cos = cos_ref[...]
    sin = sin_ref[...]
    lane = lax.broadcasted_iota(jnp.int32, cos.shape, 1)
    first_half = (lane % MLA_ROPE_DIM) < MLA_ROPE_DIM // 2
    low = lane < MLA_ROPE_DIM

    def rope(t):
        swapped = jnp.where(first_half, pltpu.roll(t, LANES - MLA_ROPE_DIM // 2, 1),
                            pltpu.roll(t, MLA_ROPE_DIM // 2, 1))
        return t * cos + swapped * sin

    scale = (MLA_NOPE_DIM + MLA_ROPE_DIM) ** -0.5 * math.log2(math.e)
    k_rope = jnp.where(low, rope(misc), 0.0).astype(BF16)
    nope_w = N_HEADS * MLA_NOPE_DIM
    for h in range(N_HEADS):
        pair = rope(mq[:, nope_w + (h // 2) * LANES: nope_w + (h // 2 + 1) * LANES])
        q_rope = pair if h % 2 == 0 else pltpu.roll(pair, MLA_ROPE_DIM, 1)
        q_rope = jnp.where(low, q_rope, 0.0)
        base = h * _MLA_QK_PAD
        q_ref[:, base:base + HEAD_DIM] = (mq[:, h * HEAD_DIM:(h + 1) * HEAD_DIM] * scale).astype(BF16)
        q_ref[:, base + HEAD_DIM:base + _MLA_QK_PAD] = (q_rope * scale).astype(BF16)
        k_ref[:, base:base + HEAD_DIM] = mkv[:, h * HEAD_DIM:(h + 1) * HEAD_DIM].astype(BF16)
        k_ref[:, base + HEAD_DIM:base + _MLA_QK_PAD] = k_rope
    v_ref[...] = mkv[:, nope_w:].T.astype(BF16)


def _mla_prep(proj, q_norm_w, kv_norm_w, w_uq, w_ukv, cos, sin, *, seq, tm=512):
    t_all = proj.shape[0]
    tm = min(tm, seq)
    nblk = seq // tm
    full = lambda a: pl.BlockSpec(a.shape, lambda i: (0,) * a.ndim)
    table = pl.BlockSpec((tm, LANES), lambda i: (i % nblk, 0))
    qnw = q_norm_w.reshape(1, -1)
    kvnw = kv_norm_w.reshape(1, -1)
    wide = N_HEADS * _MLA_QK_PAD
    return pl.pallas_call(
        _mla_prep_kernel,
        grid=(t_all // tm,),
        in_specs=[pl.BlockSpec((tm, _MLA_IN), lambda i: (i, OFF_MLA // _MLA_IN)),
                  full(qnw), full(kvnw), full(w_uq), full(w_ukv), table, table],
        out_specs=[pl.BlockSpec((tm, wide), lambda i: (i, 0)),
                   pl.BlockSpec((tm, wide), lambda i: (i, 0)),
                   pl.BlockSpec((GROUP_WIDTH, tm), lambda i: (0, i))],
        out_shape=[jax.ShapeDtypeStruct((t_all, wide), BF16),
                   jax.ShapeDtypeStruct((t_all, wide), BF16),
                   jax.ShapeDtypeStruct((GROUP_WIDTH, t_all), BF16)],
        compiler_params=_cparams("parallel"),
        name="mla_prep",
    )(proj, qnw, kvnw, w_uq, w_ukv, cos, sin)


def _mla_attn_kernel(q_ref, k_ref, vt_ref, o_ref, m_ref, l_ref, acc_ref, *, t):
    i = pl.program_id(2)
    m_ref[...] = jnp.full_like(m_ref, -jnp.inf)
    l_ref[...] = jnp.zeros_like(l_ref)
    acc_ref[...] = jnp.zeros_like(acc_ref)

    def tile(kb, diagonal):
        start = pl.multiple_of(kb * t, t)
        s = _dot_nt(k_ref[pl.ds(start, t), :], q_ref[...])
        if diagonal:
            key = lax.broadcasted_iota(jnp.int32, s.shape, 0)
            qry = lax.broadcasted_iota(jnp.int32, s.shape, 1)
            s = jnp.where(key <= qry, s, -jnp.inf)
        m_prev = m_ref[...]
        m_new = jnp.maximum(m_prev, jnp.max(s, axis=0, keepdims=True))
        p = jnp.exp2(s - m_new)
        corr = jnp.exp2(m_prev - m_new)
        l_ref[...] = corr * l_ref[...] + jnp.sum(p, axis=0, keepdims=True)
        acc_ref[...] = corr * acc_ref[...] + _dot(vt_ref[:, pl.ds(start, t)], p.astype(BF16))
        m_ref[...] = m_new

    def body(kb, carry):
        tile(kb, False)
        return carry

    lax.fori_loop(0, i, body, 0)
    tile(i, True)
    o_ref[...] = (acc_ref[...] / l_ref[...]).T.astype(o_ref.dtype)


def _mla_attn(q, k, v, *, batch, seq, t=512):
    t = min(t, seq)
    nblk = seq // t
    return pl.pallas_call(
        functools.partial(_mla_attn_kernel, t=t),
        grid=(batch, N_HEADS, nblk),
        in_specs=[pl.BlockSpec((t, _MLA_QK_PAD), lambda b, h, i: (b * nblk + i, h)),
                  pl.BlockSpec((seq, _MLA_QK_PAD), lambda b, h, i: (b, h)),
                  pl.BlockSpec((HEAD_DIM, seq), lambda b, h, i: (h, b))],
        out_specs=pl.BlockSpec((t, HEAD_DIM), lambda b, h, i: (b * nblk + i, h)),
        out_shape=jax.ShapeDtypeStruct((batch * seq, GROUP_WIDTH), BF16),
        scratch_shapes=[pltpu.VMEM((1, t), F32), pltpu.VMEM((1, t), F32), pltpu.VMEM((HEAD_DIM, t), F32)],
        compiler_params=_cparams("parallel", "parallel", "arbitrary"),
        name="mla_attn",
    )(q, k, v)


def _sgu_kernel(x_ref, nw_ref, nb_ref, ws_ref, bs_ref, o_ref, *, n_chunks):
    x = x_ref[...].astype(F32)
    ge = 0.5 * x * (1.0 + lax.erf(x * np.float32(math.sqrt(0.5))))
    u = ge[:, :GROUP_WIDTH]
    sv = ge[:, GROUP_WIDTH:]
    mu = jnp.mean(sv, axis=-1, keepdims=True)
    svc = sv - mu
    var = jnp.mean(svc * svc, axis=-1, keepdims=True)
    svn = (svc * lax.rsqrt(var + 1e-5) * nw_ref[...] + nb_ref[...]).astype(BF16)
    r_io = lax.broadcasted_iota(jnp.int32, (SGU_CHUNK, SGU_CHUNK), 0)
    c_io = lax.broadcasted_iota(jnp.int32, (SGU_CHUNK, SGU_CHUNK), 1)
    causal = r_io >= c_io
    gd = GROUP_WIDTH // SGU_GROUPS
    for g in range(SGU_GROUPS):
        w_mix = jnp.where(causal, ws_ref[g], 0.0).astype(BF16)
        bias = bs_ref[:, g:g + 1]
        cols = slice(g * gd, (g + 1) * gd)
        for c in range(n_chunks):
            rows = slice(c * SGU_CHUNK, (c + 1) * SGU_CHUNK)
            mixed = _dot(w_mix, svn[rows, cols]) + bias
            o_ref[rows, cols] = (u[rows, cols] * mixed).astype(BF16)


def _sgu(proj, norm_w, norm_b, w_s, b_s, *, rows=256):
    t_all = proj.shape[0]
    rows = min(rows, t_all)
    width = 2 * GROUP_WIDTH
    bs_t = jnp.pad(b_s.T, ((0, 0), (0, LANES - SGU_GROUPS)))
    full = lambda a: pl.BlockSpec(a.shape, lambda i: (0,) * a.ndim)
    nw = norm_w.reshape(1, -1)
    nb = norm_b.reshape(1, -1)
    return pl.pallas_call(
        functools.partial(_sgu_kernel, n_chunks=rows // SGU_CHUNK),
        grid=(t_all // rows,),
        in_specs=[pl.BlockSpec((rows, width), lambda i: (i, OFF_SGU // width)),
                  full(nw), full(nb), full(w_s), full(bs_t)],
        out_specs=pl.BlockSpec((rows, GROUP_WIDTH), lambda i: (i, 0)),
        out_shape=jax.ShapeDtypeStruct((t_all, GROUP_WIDTH), BF16),
        compiler_params=_cparams("parallel"),
        name="sgu",
    )(proj, nw, nb, w_s, bs_t)


_SB_SUB = 256
_SB_ROW_SPLIT = 2


def _sb_kernel(q_ref, k_ref, v_ref, o_ref, carry_ref, acc_ref, *, t):
    i = pl.program_id(2)
    sub = min(_SB_SUB, t)
    carry_ref[...] = jnp.zeros_like(carry_ref)
    acc_ref[...] = jnp.zeros_like(acc_ref)
    sr = lax.broadcasted_iota(jnp.int32, (2 * sub, sub), 0)
    sc = lax.broadcasted_iota(jnp.int32, (2 * sub, sub), 1)
    suffix2 = (jnp.where(sr >= sub, sr - sub, sr) > sc).astype(BF16)

    n_half = max(1, min(_SB_ROW_SPLIT, t // LANES))
    hr = t // n_half
    halves = range(n_half)
    rsl = [slice(h * hr, (h + 1) * hr) for h in halves]

    def tile(kb, diagonal):
        start = pl.multiple_of(kb * t, t)
        k = k_ref[pl.ds(start, t), :]
        z = [_dot_nt(q_ref[rsl[h], :], k) * np.float32(HEAD_DIM ** -0.5) for h in halves]
        log_beta = [jnp.minimum(zz, 0.0) - jnp.log(1.0 + jnp.exp(-jnp.abs(zz))) for zz in z]
        log_rest = [log_beta[h] - z[h] for h in halves]
        if diagonal:
            strict = [lax.broadcasted_iota(jnp.int32, (hr, t), 1) < lax.broadcasted_iota(jnp.int32, (hr, t), 0) + h * hr
                      for h in halves]
            log_rest = [jnp.where(strict[h], log_rest[h], 0.0) for h in halves]
        carry = [carry_ref[rsl[h], :] for h in halves]
        acc = [acc_ref[rsl[h], :] for h in halves]
        for sb in reversed(range(t // sub)):
            cs = slice(sb * sub, (sb + 1) * sub)
            lr = [log_rest[h][:, cs] for h in halves]
            parts = [_split2(x) for x in lr]
            tail = [_dot(jnp.concatenate(parts[h], axis=1), suffix2) for h in halves]
            a = [jnp.exp(log_beta[h][:, cs] + (tail[h] + carry[h])) for h in halves]
            if diagonal:
                a = [jnp.where(strict[h][:, cs], a[h], 0.0) for h in halves]
            v = v_ref[pl.ds(start + sb * sub, sub), :]
            acc = [acc[h] + _dot(a[h].astype(BF16), v) for h in halves]
            carry = [carry[h] + (tail[h][:, 0:1] + lr[h][:, 0:1]) for h in halves]
        for h in halves:
            carry_ref[rsl[h], :] = carry[h]
            acc_ref[rsl[h], :] = acc[h]

    tile(i, True)

    def body(jj, c):
        tile(i - 1 - jj, False)
        return c

    lax.fori_loop(0, i, body, 0)
    o_ref[...] = acc_ref[...].astype(o_ref.dtype)


def _sb_attn(proj, *, batch, seq, t=512):
    t = min(t, seq)
    nblk = seq // t
    c0 = OFF_SB // HEAD_DIM
    return pl.pallas_call(
        functools.partial(_sb_kernel, t=t),
        grid=(batch, N_HEADS, nblk),
        in_specs=[pl.BlockSpec((t, HEAD_DIM), lambda b, h, i: (b * nblk + i, c0 + h)),
                  pl.BlockSpec((seq, HEAD_DIM), lambda b, h, i: (b, c0 + N_HEADS + h)),
                  pl.BlockSpec((seq, HEAD_DIM), lambda b, h, i: (b, c0 + 2 * N_HEADS + h))],
        out_specs=pl.BlockSpec((t, HEAD_DIM), lambda b, h, i: (b * nblk + i, h)),
        out_shape=jax.ShapeDtypeStruct((batch * seq, GROUP_WIDTH), BF16),
        scratch_shapes=[pltpu.VMEM((t, 1), F32), pltpu.VMEM((t, HEAD_DIM), F32)],
        compiler_params=_cparams("parallel", "parallel", "arbitrary"),
        name="sb_attn",
    )(proj, proj, proj)


def _cast_pad_kernel(x_ref, o_ref, *, nr, nc):
    inside = (pl.program_id(0) < nr) & (pl.program_id(1) < nc)

    @pl.when(inside)
    def _():
        o_ref[...] = x_ref[...].astype(BF16)

    @pl.when(jnp.logical_not(inside))
    def _():
        o_ref[...] = jnp.zeros_like(o_ref)


def _cast_pad(w_stack, index, *, out_rows=None, out_cols=None, tr, tc, name):
    _, r, c = w_stack.shape
    out_rows = out_rows or r
    out_cols = out_cols or c
    assert r % tr == 0 and c % tc == 0 and out_rows % tr == 0 and out_cols % tc == 0
    nr, nc = r // tr, c // tc
    return pl.pallas_call(
        functools.partial(_cast_pad_kernel, nr=nr, nc=nc),
        grid=(out_rows // tr, out_cols // tc),
        in_specs=[pl.BlockSpec((None, tr, tc), lambda i, j: (index, jnp.minimum(i, nr - 1), jnp.minimum(j, nc - 1)))],
        out_specs=pl.BlockSpec((tr, tc), lambda i, j: (i, j)),
        out_shape=jax.ShapeDtypeStruct((out_rows, out_cols), BF16),
        compiler_params=_cparams("parallel", "parallel"),
        name=name,
    )(w_stack)


_W_IN_MOVES = ((0, OFF_GDN_QKV, 4096), (5584, OFF_SGU, 2048), (7632, OFF_SB, 3072), (4112, OFF_MLA, MLA_Q_RANK),
               (5008, OFF_MISC + LANES, MLA_KV_RANK))
_W_IN_MISC = ((5520, MLA_ROPE_DIM), (4096, N_HEADS), (4104, N_HEADS))


def _w_in_kernel(x_ref, o_ref):
    for src, dst, width in _W_IN_MOVES:
        o_ref[:, dst:dst + width] = x_ref[:, src:src + width].astype(BF16)
    parts = [x_ref[:, src:src + width] for src, width in _W_IN_MISC]
    used = sum(width for _, width in _W_IN_MISC)
    parts.append(jnp.zeros((x_ref.shape[0], LANES - used), F32))
    o_ref[:, OFF_MISC:OFF_MISC + LANES] = jnp.concatenate(parts, axis=1).astype(BF16)


def _w_in_bf16(w_in, layer, *, tr=128):
    _, d, n = w_in.shape
    return pl.pallas_call(
        _w_in_kernel,
        grid=(d // tr,),
        in_specs=[pl.BlockSpec((None, tr, n), lambda i: (layer, i, 0))],
        out_specs=pl.BlockSpec((tr, D_IN_PAD), lambda i: (i, 0)),
        out_shape=jax.ShapeDtypeStruct((d, D_IN_PAD), BF16),
        compiler_params=_cparams("parallel"),
        name="w_in_prep",
    )(w_in)


def _split_heads_cols(w, first):
    k = w.shape[0]
    w3 = w.reshape(k, N_HEADS, -1)
    return jnp.concatenate([w3[:, :, :first].reshape(k, -1), w3[:, :, first:].reshape(k, -1)], axis=1)


def _rope_tables(seq):
    half = MLA_ROPE_DIM // 2
    pos = jnp.arange(seq, dtype=F32)
    inv_freq = ROPE_THETA ** (-jnp.arange(half, dtype=F32) / half)
    ang = pos[:, None] * inv_freq[None, :]
    cos, sin = jnp.cos(ang), jnp.sin(ang)
    reps = LANES // half
    sign = jnp.tile(jnp.concatenate([-jnp.ones((half,), F32), jnp.ones((half,), F32)]), LANES // MLA_ROPE_DIM)
    return jnp.tile(cos, (1, reps)), jnp.tile(sin, (1, reps)) * sign[None, :]


def kernel(x, w_in, gdn_conv_w, gdn_a_log, gdn_dt_bias, gdn_norm_w, mla_q_norm_w, mla_w_uq, mla_kv_norm_w, mla_w_ukv, sgu_norm_w, sgu_norm_b, sgu_w_s, sgu_b_s, w_out, ln_mix_w, ln_mix_b, ffn_w_gate, ffn_w_up, ffn_w_down, moe_router_w, moe_router_b, moe_w_gate, moe_w_up, moe_w_down, ln_ffn_w, ln_ffn_b):
    batch, seq, d = x.shape
    t_all = batch * seq
    cos, sin = _rope_tables(seq)
    xf = x.reshape(t_all, d)
    xb = xf.astype(BF16)
    for layer in range(DEPTH):
        proj = _matmul([xb], _w_in_bf16(w_in, layer), tm=1024, tn=768, tk=d, out_dtype=BF16, name="in_proj")
        qkv = _gdn_prep(proj, gdn_conv_w[layer], batch=batch, seq=seq)
        o_a = _gdn(qkv, proj, gdn_a_log[layer], gdn_dt_bias[layer], gdn_norm_w[layer], batch=batch, seq=seq)
        mq, mk, mv = _mla_prep(proj, mla_q_norm_w[layer], mla_kv_norm_w[layer],
                               _split_heads_cols(mla_w_uq[layer], MLA_NOPE_DIM).astype(BF16),
                               _split_heads_cols(mla_w_ukv[layer], MLA_NOPE_DIM).astype(BF16),
                               cos, sin, seq=seq)
        o_b = _mla_attn(mq, mk, mv, batch=batch, seq=seq)
        o_c = _sgu(proj, sgu_norm_w[layer], sgu_norm_b[layer], sgu_w_s[layer], sgu_b_s[layer])
        o_d = _sb_attn(proj, batch=batch, seq=seq)
        w_out_bf = _cast_pad(w_out, layer, tr=512, tc=2048, name="w_out_prep")
        pre = _matmul([o_a, o_b, o_c, o_d], w_out_bf, tm=1024, tn=512, tk=GROUP_WIDTH,
                      out_dtype=F32, res=xf, alpha=DEEPNORM_ALPHA, name="out_proj")
        xf, xb = _layer_norm(pre, ln_mix_w[layer], ln_mix_b[layer], name="ln_mix")
        i = layer // 2
        if layer % 2 == 0:
            wg = _cast_pad(ffn_w_gate, i, out_cols=D_FF_PAD, tr=2048, tc=256, name="ffn_w_prep")
            wu = _cast_pad(ffn_w_up, i, out_cols=D_FF_PAD, tr=2048, tc=256, name="ffn_w_prep")
            wd = _cast_pad(ffn_w_down, i, out_rows=D_FF_PAD, tr=256, tc=d, name="ffn_w_prep")
            hid = _gate_up(xb, wg, wu, tm=1024, tn=512, name="ffn_gate_up")
            pre = _matmul([hid], wd, tm=1024, tn=1024, tk=2816, out_dtype=F32, res=xf, alpha=DEEPNORM_ALPHA,
                          name="ffn_down")
            xf, xb = _layer_norm(pre, ln_ffn_w[layer], ln_ffn_b[layer], name="ln_ffn")
        else:
            meta, counts = _router(xf, moe_router_w[i], moe_router_b[i])
            pos1, pos2, src, wrow, tile_expert = _moe_plan(meta, counts, t_all)
            xs = _moe_gather(xf, src)
            n_moe = moe_w_gate.shape[0]
            e_rows = N_EXPERTS * d
            wg = _cast_pad(moe_w_gate.reshape(n_moe, e_rows, D_EXPERT), i, tr=1024, tc=896, name="moe_w_prep")
            wu = _cast_pad(moe_w_up.reshape(n_moe, e_rows, D_EXPERT), i, tr=1024, tc=896, name="moe_w_prep")
            wd = _cast_pad(moe_w_down.reshape(n_moe, N_EXPERTS * D_EXPERT, d), i, tr=512, tc=2048, name="moe_w_prep")
            hid = _moe_up(xs, wg.reshape(N_EXPERTS, d, D_EXPERT), wu.reshape(N_EXPERTS, d, D_EXPERT), wrow, tile_expert)
            ys = _moe_down(hid, wd.reshape(N_EXPERTS, D_EXPERT, d), tile_expert)
            xf, xb = _moe_combine(ys, pos1, pos2, xf, ln_ffn_w[layer], ln_ffn_b[layer])
    return xf.reshape(batch, seq, d)
```
